```python
import math
import jax, jax.numpy as jnp
from jax import lax
import numpy as np

D_MODEL = 1024
BATCH = 2
SEQ = 8192
DEPTH = 1
DEC_BATCH = 4
DEC_SEQ = 4096
PAST_LEN = 128

HEAD_DIM = 64
N_HEADS_A = 8
N_HEADS_B = 8
N_KV_B = 2
GQA_GROUP = N_HEADS_B // N_KV_B
WIDTH_A = N_HEADS_A * HEAD_DIM
WIDTH_B = N_HEADS_B * HEAD_DIM
KV_WIDTH_B = N_KV_B * HEAD_DIM
MIX_WIDTH = WIDTH_A + WIDTH_B
IN_WIDTH = 3 * WIDTH_A + WIDTH_B + 2 * KV_WIDTH_B
DILATED_PATTERNS = ((128, 1), (512, 4), (2048, 16))
ROPE_THETA_A = 500000.0
ROT_DIM_A = HEAD_DIM // 4
ROPE_THETA_B = 10000.0
GRID_W = 64
Q_BLOCK = 128
N_EXPERTS = 16
EC_CAPACITY_FACTOR = 2
EXPERT_FF = 2816
EPS = 1e-6
NEG_INF = -1e30

kernel_name = "hybrid_dilated_axial_gqa_ec_encoder"


def rms_norm(x, g):
    xf = x.astype(jnp.float32)
    y = xf * lax.rsqrt(jnp.mean(xf * xf, axis=-1, keepdims=True) + EPS)
    return (y * g.astype(jnp.float32)).astype(x.dtype)


def rope_cos_sin(pos, dim, theta):
    inv = theta ** (-jnp.arange(0, dim, 2, dtype=jnp.float32) / dim)
    ang = pos.astype(jnp.float32)[:, None] * inv[None, :]
    return jnp.cos(ang), jnp.sin(ang)


def apply_rope(x, cos, sin):
    half = x.shape[-1] // 2
    x1 = x[..., :half].astype(jnp.float32)
    x2 = x[..., half:].astype(jnp.float32)
    c = cos[None, :, None, :]
    s = sin[None, :, None, :]
    return jnp.concatenate([x1 * c - x2 * s, x1 * s + x2 * c], axis=-1).astype(x.dtype)


def dilated_mixture_attention(q, k, v):
    B, S, H, Dh = q.shape
    scale = Dh ** -0.5
    padded = []
    for window, dil in DILATED_PATTERNS:
        half = window // 2 // dil
        pad = half * dil
        kp = jnp.pad(k, ((0, 0), (pad, pad), (0, 0), (0, 0)))
        vp = jnp.pad(v, ((0, 0), (pad, pad), (0, 0), (0, 0)))
        padded.append((kp, vp, dil, half))

    def block(bi):
        s0 = bi * Q_BLOCK
        qpos = s0 + jnp.arange(Q_BLOCK)
        qb = lax.dynamic_slice_in_dim(q, s0, Q_BLOCK, axis=1)
        outs, lses = [], []
        for kp, vp, dil, half in padded:
            kpos = qpos[:, None] + dil * jnp.arange(-half, half + 1)[None, :]
            valid = (kpos >= 0) & (kpos < S)
            kb = jnp.take(kp, kpos + half * dil, axis=1)
            vb = jnp.take(vp, kpos + half * dil, axis=1)
            s = jnp.einsum('bqhd,bqkhd->bqhk', qb, kb, preferred_element_type=jnp.float32) * scale
            s = jnp.where(valid[None, :, None, :], s, NEG_INF)
            m = jnp.max(s, axis=-1, keepdims=True)
            p = jnp.exp(s - m)
            l = jnp.sum(p, axis=-1, keepdims=True)
            o = jnp.einsum('bqhk,bqkhd->bqhd', (p / l).astype(v.dtype), vb, preferred_element_type=jnp.float32)
            outs.append(o)
            lses.append((m + jnp.log(l))[..., 0])
        w = jax.nn.softmax(jnp.stack(lses), axis=0)
        o = jnp.sum(w[..., None] * jnp.stack(outs), axis=0)
        return o.astype(q.dtype)

    ob = lax.map(block, jnp.arange(S // Q_BLOCK))
    return jnp.moveaxis(ob, 0, 1).reshape(B, S, H * Dh)


def axial_gqa_attention(q, k, v):
    B, S, Hq, Dh = q.shape
    scale = Dh ** -0.5
    qg = q.reshape(B, S, N_KV_B, GQA_GROUP, Dh)

    def block(bi):
        qb = lax.dynamic_slice_in_dim(qg, bi * Q_BLOCK, Q_BLOCK, axis=1)
        s = jnp.einsum('bqngd,bknd->bngqk', qb, k, preferred_element_type=jnp.float32) * scale
        p = jax.nn.softmax(s, axis=-1)
        return jnp.einsum('bngqk,bknd->bqngd', p.astype(v.dtype), v)

    ob = lax.map(block, jnp.arange(S // Q_BLOCK))
    return jnp.moveaxis(ob, 0, 1).reshape(B, S, Hq * Dh)


def hybrid_mixer(h, w_in, q_norm_b, k_norm_b, norm_out_a, norm_out_b, w_out):
    B, S, _ = h.shape
    proj = jnp.matmul(h, w_in)
    o1 = WIDTH_A
    o2 = 2 * WIDTH_A
    o3 = 3 * WIDTH_A
    o4 = o3 + WIDTH_B
    o5 = o4 + KV_WIDTH_B
    qa, ka, va, qb, kb, vb = jnp.split(proj, [o1, o2, o3, o4, o5], axis=-1)
    qa = qa.reshape(B, S, N_HEADS_A, HEAD_DIM)
    ka = ka.reshape(B, S, N_HEADS_A, HEAD_DIM)
    va = va.reshape(B, S, N_HEADS_A, HEAD_DIM)
    cos_a, sin_a = rope_cos_sin(jnp.arange(S), ROT_DIM_A, ROPE_THETA_A)
    qa = jnp.concatenate([apply_rope(qa[..., :ROT_DIM_A], cos_a, sin_a), qa[..., ROT_DIM_A:]], axis=-1)
    ka = jnp.concatenate([apply_rope(ka[..., :ROT_DIM_A], cos_a, sin_a), ka[..., ROT_DIM_A:]], axis=-1)
    out_a = dilated_mixture_attention(qa, ka, va)
    qb = rms_norm(qb.reshape(B, S, N_HEADS_B, HEAD_DIM), q_norm_b)
    kb = rms_norm(kb.reshape(B, S, N_KV_B, HEAD_DIM), k_norm_b)
    vb = vb.reshape(B, S, N_KV_B, HEAD_DIM)
    rows = S // GRID_W
    row_idx = jnp.repeat(jnp.arange(rows), GRID_W)
    col_idx = jnp.tile(jnp.arange(GRID_W), rows)
    half = HEAD_DIM // 2
    cos_r, sin_r = rope_cos_sin(row_idx, half, ROPE_THETA_B)
    cos_c, sin_c = rope_cos_sin(col_idx, half, ROPE_THETA_B)
    qb = jnp.concatenate([apply_rope(qb[..., :half], cos_r, sin_r), apply_rope(qb[..., half:], cos_c, sin_c)], axis=-1)
    kb = jnp.concatenate([apply_rope(kb[..., :half], cos_r, sin_r), apply_rope(kb[..., half:], cos_c, sin_c)], axis=-1)
    out_b = axial_gqa_attention(qb, kb, vb)
    merged = jnp.concatenate([rms_norm(out_a, norm_out_a), rms_norm(out_b, norm_out_b)], axis=-1)
    return jnp.matmul(merged, w_out)


def expert_choice_ffn(h, w_router, w_gate, w_up, w_down):
    B, S, D = h.shape
    n_tok = B * S
    cap = EC_CAPACITY_FACTOR * n_tok // N_EXPERTS
    xt = h.reshape(n_tok, D)
    aff = jax.nn.softmax(jnp.matmul(xt, w_router, preferred_element_type=jnp.float32), axis=-1)
    gates, idx = lax.top_k(aff.T, cap)
    xe = jnp.take(xt, idx, axis=0)
    hid = jax.nn.silu(jnp.einsum('ecd,edf->ecf', xe, w_gate)) * jnp.einsum('ecd,edf->ecf', xe, w_up)
    ye = jnp.einsum('ecf,efd->ecd', hid, w_down) * gates[..., None].astype(h.dtype)
    out = jnp.zeros_like(xt).at[idx.reshape(-1)].add(ye.reshape(-1, D))
    return out.reshape(B, S, D)


def encoder_trunk(x, norm_mix, w_in, q_norm_b, k_norm_b, norm_out_a, norm_out_b, w_out,
                  norm_ffn, w_router, w_gate, w_up, w_down, norm_final):
    for l in range(DEPTH):
        x = x + hybrid_mixer(rms_norm(x, norm_mix[l]), w_in[l], q_norm_b[l], k_norm_b[l],
                             norm_out_a[l], norm_out_b[l], w_out[l])
        x = x + expert_choice_ffn(rms_norm(x, norm_ffn[l]), w_router[l], w_gate[l], w_up[l], w_down[l])
    return rms_norm(x, norm_final)


def setup_inputs(seed: int = 0) -> dict:
    key = jax.random.key(seed)
    ks = jax.random.split(key, 16)
    f32 = jnp.float32

    def gain(k, shape):
        return 1.0 + 0.02 * jax.random.normal(k, shape, f32)

    return {
        "x_prompt": jax.random.normal(ks[0], (BATCH, SEQ, D_MODEL), f32),
        "x_sample": jax.random.normal(ks[1], (DEC_BATCH, DEC_SEQ, D_MODEL), f32),
        "norm_mix": gain(ks[2], (DEPTH, D_MODEL)),
        "w_in": jax.random.normal(ks[3], (DEPTH, D_MODEL, IN_WIDTH), f32) * D_MODEL ** -0.5,
        "q_norm_b": gain(ks[4], (DEPTH, HEAD_DIM)),
        "k_norm_b": gain(ks[5], (DEPTH, HEAD_DIM)),
        "norm_out_a": gain(ks[6], (DEPTH, WIDTH_A)),
        "norm_out_b": gain(ks[7], (DEPTH, WIDTH_B)),
        "w_out": jax.random.normal(ks[8], (DEPTH, MIX_WIDTH, D_MODEL), f32) * MIX_WIDTH ** -0.5,
        "norm_ffn": gain(ks[9], (DEPTH, D_MODEL)),
        "w_router": jax.random.normal(ks[10], (DEPTH, D_MODEL, N_EXPERTS), f32) * D_MODEL ** -0.5,
        "w_gate": jax.random.normal(ks[11], (DEPTH, N_EXPERTS, D_MODEL, EXPERT_FF), f32) * D_MODEL ** -0.5,
        "w_up": jax.random.normal(ks[12], (DEPTH, N_EXPERTS, D_MODEL, EXPERT_FF), f32) * D_MODEL ** -0.5,
        "w_down": jax.random.normal(ks[13], (DEPTH, N_EXPERTS, EXPERT_FF, D_MODEL), f32) * EXPERT_FF ** -0.5,
        "norm_final": gain(ks[14], (D_MODEL,)),
    }


def reference(x_prompt, x_sample, norm_mix, w_in, q_norm_b, k_norm_b, norm_out_a, norm_out_b, w_out,
              norm_ffn, w_router, w_gate, w_up, w_down, norm_final):
    y_prompt = encoder_trunk(x_prompt, norm_mix, w_in, q_norm_b, k_norm_b, norm_out_a, norm_out_b, w_out,
                             norm_ffn, w_router, w_gate, w_up, w_down, norm_final)
    y_sample = encoder_trunk(x_sample, norm_mix, w_in, q_norm_b, k_norm_b, norm_out_a, norm_out_b, w_out,
                             norm_ffn, w_router, w_gate, w_up, w_down, norm_final)
    return (y_prompt, y_sample)
```

```python
import functools

import jax
import jax.numpy as jnp
from jax import lax
from jax.experimental import pallas as pl
from jax.experimental.pallas import tpu as pltpu

F32 = jnp.float32
BF16 = jnp.bfloat16

D_MODEL = 1024
HEAD_DIM = 64
N_HEADS_A = 8
N_HEADS_B = 8
N_KV_B = 2
WIDTH_A = N_HEADS_A * HEAD_DIM
WIDTH_B = N_HEADS_B * HEAD_DIM
KV_WIDTH_B = N_KV_B * HEAD_DIM
IN_WIDTH = 3 * WIDTH_A + WIDTH_B + 2 * KV_WIDTH_B
DILATIONS = (1, 4, 16)
BAND_HALF = 64
ROPE_THETA_A = 500000.0
ROT_DIM_A = HEAD_DIM // 4
ROPE_THETA_B = 10000.0
GRID_W = 64
N_EXPERTS = 16
EC_CAPACITY_FACTOR = 2
EXPERT_FF = 2816
EPS = 1e-6
NEG_INF = -1e30
SCALE = HEAD_DIM ** -0.5

LANES = 128
ROW_TILE = 256
Q_TILE = 128
QB_TILE = 128
KB_TILE = 512
FF_TILE = 256
AFF_PAD = LANES
VMEM_LIMIT = 56 * 1024 * 1024


def _cparams(sem):
    return pltpu.CompilerParams(dimension_semantics=sem, vmem_limit_bytes=VMEM_LIMIT)


def _rms(x, g):
    return x * lax.rsqrt(jnp.mean(x * x, axis=-1, keepdims=True) + EPS) * g


def _dot(a, b):
    return jnp.dot(a, b, preferred_element_type=F32)


def _dot_nt(a, b):
    return lax.dot_general(a, b, (((1,), (1,)), ((), ())), preferred_element_type=F32)


def _rope(v, c, s_next, s_prev, shift):
    return v * c + pltpu.roll(v, LANES - shift, 1) * s_next + pltpu.roll(v, shift, 1) * s_prev


def _head_norm(v, gain, ones_bd):
    sq = v * v
    hi = sq.astype(BF16)
    lo = (sq - hi.astype(F32)).astype(BF16)
    ss = _dot(hi, ones_bd) + _dot(lo, ones_bd)
    return v * lax.rsqrt(ss * (1.0 / HEAD_DIM) + EPS) * gain


def _proj_kernel(x_ref, g_ref, w_ref, qn_ref, kn_ref, ca_ref, sa1_ref, sa2_ref, cb_ref, sb1_ref, sb2_ref,
                 ones_ref, qa_ref, ka_ref, va_ref, qb_ref, kb_ref, vb_ref):
    h = _rms(x_ref[...], g_ref[...]).astype(BF16)
    ca, sa1, sa2 = ca_ref[...], sa1_ref[...], sa2_ref[...]
    cb, sb1, sb2 = cb_ref[...], sb1_ref[...], sb2_ref[...]
    ones_bd = ones_ref[...]
    o_ka, o_va, o_qb, o_kb, o_vb = WIDTH_A, 2 * WIDTH_A, 3 * WIDTH_A, 3 * WIDTH_A + WIDTH_B, 3 * WIDTH_A + WIDTH_B + KV_WIDTH_B

    p = _dot(h, w_ref[:, 0:o_ka])
    for c in range(WIDTH_A // LANES):
        sl = slice(c * LANES, (c + 1) * LANES)
        qa_ref[:, sl] = (_rope(p[:, sl], ca, sa1, sa2, ROT_DIM_A // 2) * SCALE).astype(BF16)
    p = _dot(h, w_ref[:, o_ka:o_va])
    for c in range(WIDTH_A // LANES):
        sl = slice(c * LANES, (c + 1) * LANES)
        ka_ref[:, sl] = _rope(p[:, sl], ca, sa1, sa2, ROT_DIM_A // 2).astype(BF16)
    va_ref[...] = _dot(h, w_ref[:, o_va:o_qb]).astype(BF16)

    p = _dot(h, w_ref[:, o_qb:o_kb])
    qn = qn_ref[...]
    for c in range(WIDTH_B // LANES):
        sl = slice(c * LANES, (c + 1) * LANES)
        v = _head_norm(p[:, sl], qn, ones_bd)
        qb_ref[:, sl] = (_rope(v, cb, sb1, sb2, HEAD_DIM // 4) * SCALE).astype(BF16)
    p = _dot(h, w_ref[:, o_kb:o_vb])
    v = _head_norm(p, kn_ref[...], ones_bd)
    kb_ref[...] = _rope(v, cb, sb1, sb2, HEAD_DIM // 4).astype(BF16)
    vb_ref[...] = _dot(h, w_ref[:, o_vb:IN_WIDTH]).astype(BF16)


def _rope_tables(seq):
    lane = jnp.arange(LANES) % HEAD_DIM
    pos = jnp.arange(seq)

    def cos_sin(p, dim, theta):
        inv = theta ** (-jnp.arange(0, dim, 2, dtype=F32) / dim)
        ang = p.astype(F32)[:, None] * inv[None, :]
        return jnp.cos(ang), jnp.sin(ang)

    ha = ROT_DIM_A // 2
    cos_a, sin_a = cos_sin(pos, ROT_DIM_A, ROPE_THETA_A)
    fa = lane % ha
    in_lo, in_hi = lane < ha, (lane >= ha) & (lane < ROT_DIM_A)
    ca = jnp.where((in_lo | in_hi)[None, :], cos_a[:, fa], 1.0)
    sa1 = jnp.where(in_lo[None, :], -sin_a[:, fa], 0.0)
    sa2 = jnp.where(in_hi[None, :], sin_a[:, fa], 0.0)
    hb = HEAD_DIM // 4
    cos_r, sin_r = cos_sin(pos // GRID_W, HEAD_DIM // 2, ROPE_THETA_B)
    cos_c, sin_c = cos_sin(pos % GRID_W, HEAD_DIM // 2, ROPE_THETA_B)
    fb = lane % hb
    is_row = (lane < HEAD_DIM // 2)[None, :]
    cosb = jnp.where(is_row, cos_r[:, fb], cos_c[:, fb])
    sinb = jnp.where(is_row, sin_r[:, fb], sin_c[:, fb])
    first = ((lane // hb) % 2 == 0)[None, :]
    sb1 = jnp.where(first, -sinb, 0.0)
    sb2 = jnp.where(first, 0.0, sinb)
    return ca, sa1, sa2, cosb, sb1, sb2


def _project(x2d, seq, norm_mix, w_in_bf, q_norm, k_norm, tables):
    n = x2d.shape[0]
    tm = ROW_TILE
    tiles_per_seq = seq // tm
    row = lambda i: (i, 0)
    fixed = lambda i: (0, 0)
    tab = lambda i: (i % tiles_per_seq, 0)
    ones_bd = (jnp.arange(LANES)[:, None] // HEAD_DIM == jnp.arange(LANES)[None, :] // HEAD_DIM).astype(BF16)
    gain2 = lambda g: jnp.tile(g.reshape(1, HEAD_DIM), (1, LANES // HEAD_DIM))
    tab_spec = pl.BlockSpec((tm, LANES), tab)
    out_shapes = (
        jax.ShapeDtypeStruct((n, WIDTH_A), BF16), jax.ShapeDtypeStruct((n, WIDTH_A), BF16),
        jax.ShapeDtypeStruct((n, WIDTH_A), BF16), jax.ShapeDtypeStruct((n, WIDTH_B), BF16),
        jax.ShapeDtypeStruct((n, KV_WIDTH_B), BF16), jax.ShapeDtypeStruct((n, KV_WIDTH_B), BF16))
    return pl.pallas_call(
        _proj_kernel,
        grid=(n // tm,),
        in_specs=[pl.BlockSpec((tm, D_MODEL), row), pl.BlockSpec((1, D_MODEL), fixed),
                  pl.BlockSpec((D_MODEL, IN_WIDTH), fixed), pl.BlockSpec((1, LANES), fixed),
                  pl.BlockSpec((1, LANES), fixed)] + [tab_spec] * 6 + [pl.BlockSpec((LANES, LANES), fixed)],
        out_specs=[pl.BlockSpec((tm, WIDTH_A), row)] * 3 + [pl.BlockSpec((tm, WIDTH_B), row)]
        + [pl.BlockSpec((tm, KV_WIDTH_B), row)] * 2,
        out_shape=out_shapes,
        name="in_proj",
        compiler_params=_cparams(("parallel",)),
    )(x2d, norm_mix.reshape(1, D_MODEL), w_in_bf, gain2(q_norm), gain2(k_norm), *tables, ones_bd)


def _band_kernel(q_ref, kp_ref, kc_ref, kn_ref, vp_ref, vc_ref, vn_ref, o_ref, lse_ref, *, sub_len):
    i = pl.program_id(2)
    t = Q_TILE
    q = q_ref[0]
    kw = jnp.concatenate([kp_ref[0, t - BAND_HALF:t], kc_ref[0], kn_ref[0, 0:BAND_HALF]], axis=0)
    vw = jnp.concatenate([vp_ref[0, t - BAND_HALF:t], vc_ref[0], vn_ref[0, 0:BAND_HALF]], axis=0)
    nk = t + 2 * BAND_HALF
    row = lax.broadcasted_iota(jnp.int32, (t, nk), 0)
    col = lax.broadcasted_iota(jnp.int32, (t, nk), 1)
    kj = i * t - BAND_HALF + col
    rel = col - BAND_HALF - row
    valid = (kj >= 0) & (kj < sub_len) & (rel >= -BAND_HALF) & (rel <= BAND_HALF)
    lo_half = lax.broadcasted_iota(jnp.int32, (t, LANES), 1) < HEAD_DIM
    zero = jnp.zeros((), BF16)
    for c in range(WIDTH_A // LANES):
        sl = slice(c * LANES, (c + 1) * LANES)
        qc, kc, vc = q[:, sl], kw[:, sl], vw[:, sl]
        outs, lses = [], []
        for half_mask in (lo_half, jnp.logical_not(lo_half)):
            s = _dot_nt(jnp.where(half_mask, qc, zero), kc)
            s = jnp.where(valid, s, NEG_INF)
            m = jnp.max(s, axis=-1, keepdims=True)
            p = jnp.exp(s - m)
            l = jnp.sum(p, axis=-1, keepdims=True)
            outs.append(_dot(p.astype(BF16), vc) / l)
            lses.append(jnp.broadcast_to(m + jnp.log(l), (t, LANES)))
        o_ref[0, :, sl] = jnp.where(lo_half, outs[0], outs[1])
        lse_ref[0, :, sl] = jnp.where(lo_half, lses[0], lses[1])


def _banded_attention(q, k, v, batch, seq, dil):
    sub_len = seq // dil
    nb = sub_len // Q_TILE
    view = lambda a: a.reshape(batch, sub_len, dil * WIDTH_A)
    blk = (1, Q_TILE, WIDTH_A)
    cur = pl.BlockSpec(blk, lambda b, r, i: (b, i, r))
    prev = pl.BlockSpec(blk, lambda b, r, i: (b, jnp.maximum(i - 1, 0), r))
    nxt = pl.BlockSpec(blk, lambda b, r, i: (b, jnp.minimum(i + 1, nb - 1), r))
    shape = jax.ShapeDtypeStruct((batch, sub_len, dil * WIDTH_A), F32)
    o, lse = pl.pallas_call(
        functools.partial(_band_kernel, sub_len=sub_len),
        grid=(batch, dil, nb),
        in_specs=[cur, prev, cur, nxt, prev, cur, nxt],
        out_specs=[cur, cur],
        out_shape=(shape, shape),
        name=f"band_attn_d{dil}",
        compiler_params=_cparams(("parallel", "parallel", "parallel")),
    )(view(q), view(k), view(k), view(k), view(v), view(v), view(v))
    return o.reshape(batch * seq, WIDTH_A), lse.reshape(batch * seq, WIDTH_A)


def _gqa_kernel(q_ref, k_ref, v_ref, o_ref, qs_ref, m_ref, l_ref, acc_ref, *, seq):
    tq = QB_TILE
    group = N_HEADS_B // N_KV_B
    lane_half = lax.broadcasted_iota(jnp.int32, (tq, LANES), 1) // HEAD_DIM
    heads_out = [None] * N_HEADS_B
    for n in range(N_KV_B):
        for g in range(group):
            hd = n * group + g
            qc = q_ref[0, :, (hd // 2) * LANES:(hd // 2 + 1) * LANES].astype(F32)
            if hd % 2 != n:
                qc = pltpu.roll(qc, HEAD_DIM, 1)
            qs_ref[g * tq:(g + 1) * tq, :] = jnp.where(lane_half == n, qc, 0.0).astype(BF16)
        m_ref[...] = jnp.full(m_ref.shape, -jnp.inf, F32)
        l_ref[...] = jnp.zeros(l_ref.shape, F32)
        acc_ref[...] = jnp.zeros(acc_ref.shape, F32)

        def kv_step(j, carry):
            start = pl.multiple_of(j * KB_TILE, KB_TILE)
            kc = k_ref[0, pl.ds(start, KB_TILE), :]
            vc = v_ref[0, pl.ds(start, KB_TILE), :]
            s = _dot_nt(qs_ref[...], kc)
            m_old = m_ref[...]
            m_new = jnp.maximum(m_old, jnp.max(s, axis=-1, keepdims=True))
            alpha = jnp.exp(m_old - m_new)
            p = jnp.exp(s - m_new)
            l_ref[...] = alpha * l_ref[...] + jnp.sum(p, axis=-1, keepdims=True)
            acc_ref[...] = alpha * acc_ref[...] + _dot(p.astype(BF16), vc)
            m_ref[...] = m_new
            return carry

        lax.fori_loop(0, seq // KB_TILE, kv_step, 0)
        res = acc_ref[...] / l_ref[...]
        for g in range(group):
            heads_out[n * group + g] = res[g * tq:(g + 1) * tq, :]
    for c in range(WIDTH_B // LANES):
        even, odd = heads_out[2 * c], heads_out[2 * c + 1]
        n = (2 * c) // group
        if n == 0:
            odd = pltpu.roll(odd, HEAD_DIM, 1)
        else:
            even = pltpu.roll(even, HEAD_DIM, 1)
        o_ref[0, :, c * LANES:(c + 1) * LANES] = jnp.where(lane_half == 0, even, odd)


def _gqa_attention(q, k, v, batch, seq):
    tq = QB_TILE
    rows = (N_HEADS_B // N_KV_B) * tq
    o = pl.pallas_call(
        functools.partial(_gqa_kernel, seq=seq),
        grid=(batch, seq // tq),
        in_specs=[pl.BlockSpec((1, tq, WIDTH_B), lambda b, i: (b, i, 0)),
                  pl.BlockSpec((1, seq, KV_WIDTH_B), lambda b, i: (b, 0, 0)),
                  pl.BlockSpec((1, seq, KV_WIDTH_B), lambda b, i: (b, 0, 0))],
        out_specs=pl.BlockSpec((1, tq, WIDTH_B), lambda b, i: (b, i, 0)),
        out_shape=jax.ShapeDtypeStruct((batch, seq, WIDTH_B), F32),
        scratch_shapes=[pltpu.VMEM((rows, LANES), BF16), pltpu.VMEM((rows, 1), F32),
                        pltpu.VMEM((rows, 1), F32), pltpu.VMEM((rows, LANES), F32)],
        name="gqa_attn",
        compiler_params=_cparams(("parallel", "parallel")),
    )(q.reshape(batch, seq, WIDTH_B), k.reshape(batch, seq, KV_WIDTH_B), v.reshape(batch, seq, KV_WIDTH_B))
    return o.reshape(batch * seq, WIDTH_B)


def _post_kernel(o1_ref, o2_ref, o3_ref, l1_ref, l2_ref, l3_ref, ob_ref, x_ref, ga_ref, gb_ref, wo_ref,
                 gf_ref, wr_ref, x1_ref, h2_ref, afft_ref):
    l1, l2, l3 = l1_ref[...], l2_ref[...], l3_ref[...]
    mx = jnp.maximum(jnp.maximum(l1, l2), l3)
    e1, e2, e3 = jnp.exp(l1 - mx), jnp.exp(l2 - mx), jnp.exp(l3 - mx)
    oa = (e1 * o1_ref[...] + e2 * o2_ref[...] + e3 * o3_ref[...]) / (e1 + e2 + e3)
    na = _rms(oa, ga_ref[...]).astype(BF16)
    nb = _rms(ob_ref[...], gb_ref[...]).astype(BF16)
    mixed = _dot(na, wo_ref[0:WIDTH_A, :]) + _dot(nb, wo_ref[WIDTH_A:WIDTH_A + WIDTH_B, :])
    x1 = x_ref[...] + mixed
    x1_ref[...] = x1
    h2 = _rms(x1, gf_ref[...])
    logits = _dot(h2.astype(BF16), wr_ref[...])
    lane = lax.broadcasted_iota(jnp.int32, logits.shape, 1)
    logits = jnp.where(lane < N_EXPERTS, logits, -jnp.inf)
    ex = jnp.exp(logits - jnp.max(logits, axis=-1, keepdims=True))
    aff = ex / jnp.sum(ex, axis=-1, keepdims=True)
    h2_ref[:, 0:D_MODEL] = h2
    h2_ref[:, D_MODEL:D_MODEL + AFF_PAD] = aff
    afft_ref[...] = aff.T[0:N_EXPERTS, :]


def _post_attention(oas, lses, ob, x2d, norm_out_a, norm_out_b, w_out_bf, norm_ffn, w_router_bf):
    n = x2d.shape[0]
    tm = ROW_TILE
    row = lambda i: (i, 0)
    fixed = lambda i: (0, 0)
    half = pl.BlockSpec((tm, WIDTH_A), row)
    wr_pad = jnp.zeros((D_MODEL, AFF_PAD), BF16).at[:, :N_EXPERTS].set(w_router_bf)
    return pl.pallas_call(
        _post_kernel,
        grid=(n // tm,),
        in_specs=[half] * 7 + [pl.BlockSpec((tm, D_MODEL), row), pl.BlockSpec((1, WIDTH_A), fixed),
                               pl.BlockSpec((1, WIDTH_B), fixed), pl.BlockSpec((WIDTH_A + WIDTH_B, D_MODEL), fixed),
                               pl.BlockSpec((1, D_MODEL), fixed), pl.BlockSpec((D_MODEL, AFF_PAD), fixed)],
        out_specs=[pl.BlockSpec((tm, D_MODEL), row), pl.BlockSpec((tm, D_MODEL + AFF_PAD), row),
                   pl.BlockSpec((N_EXPERTS, tm), lambda i: (0, i))],
        out_shape=(jax.ShapeDtypeStruct((n, D_MODEL), F32), jax.ShapeDtypeStruct((n, D_MODEL + AFF_PAD), F32),
                   jax.ShapeDtypeStruct((N_EXPERTS, n), F32)),
        name="post_attn",
        compiler_params=_cparams(("parallel",)),
    )(*oas, *lses, ob, x2d, norm_out_a.reshape(1, WIDTH_A), norm_out_b.reshape(1, WIDTH_B), w_out_bf,
      norm_ffn.reshape(1, D_MODEL), wr_pad)


def _select_kernel(aff_ref, idx_ref, *, cap):
    ne, nch, _ = aff_ref.shape
    bits = pltpu.bitcast(aff_ref[...], jnp.int32)

    def count(mask):
        return jnp.sum(jnp.sum(mask.astype(F32), axis=1, keepdims=True), axis=2, keepdims=True)

    def search(i, thr):
        cand = thr | jnp.left_shift(jnp.int32(1), 30 - i)
        return jnp.where(count(bits >= cand) >= cap, cand, thr)

    thr = lax.fori_loop(0, 31, search, jnp.zeros((ne, 1, 1), jnp.int32))

    ci = lax.broadcasted_iota(jnp.int32, (LANES, LANES), 0)
    cj = lax.broadcasted_iota(jnp.int32, (LANES, LANES), 1)
    upper = (ci <= cj).astype(BF16)
    ones = jnp.ones((LANES, LANES), BF16)
    ri = lax.broadcasted_iota(jnp.int32, (nch, nch), 0)
    rj = lax.broadcasted_iota(jnp.int32, (nch, nch), 1)
    strict_lower = (rj < ri).astype(BF16)

    def prefix(mask_f):
        mb = mask_f.astype(BF16).reshape(ne * nch, LANES)
        local = _dot(mb, upper).reshape(ne, nch, LANES)
        tot = _dot(mb, ones).reshape(ne, nch, LANES)
        offs = jnp.stack([_dot(strict_lower, tot[e].astype(BF16)) for e in range(ne)])
        return local, tot, offs

    gt = bits > thr
    eq = bits == thr
    need = cap - count(gt)
    eq_local, _, eq_offs = prefix(eq.astype(F32))
    sel = jnp.logical_or(gt, jnp.logical_and(eq, (eq_local + eq_offs) <= need)).astype(F32)
    loc, tot, offs = prefix(sel)

    slot = lax.broadcasted_iota(jnp.int32, (1, cap), 1).astype(F32)
    chunk_id = lax.broadcasted_iota(jnp.int32, (nch, 1), 0).astype(F32)
    for e in range(ne):
        before = offs[e][:, 0:1]
        through = before + tot[e][:, 0:1]
        chunk_of = jnp.sum((through <= slot).astype(F32), axis=0, keepdims=True)
        onehot = chunk_id == chunk_of
        base = jnp.sum(jnp.where(onehot, before, 0.0), axis=0, keepdims=True)
        rows = _dot(loc[e].T.astype(BF16), onehot.astype(BF16))
        lane_of = jnp.sum((rows <= slot - base).astype(F32), axis=0, keepdims=True)
        idx_ref[e:e + 1, :] = (chunk_of * LANES + lane_of).astype(jnp.int32)


def _select(aff_t, cap):
    ne, n = aff_t.shape
    return pl.pallas_call(
        functools.partial(_select_kernel, cap=cap),
        out_shape=jax.ShapeDtypeStruct((ne, cap), jnp.int32),
        name="ec_select",
        compiler_params=pltpu.CompilerParams(vmem_limit_bytes=VMEM_LIMIT),
    )(aff_t.reshape(ne, n // LANES, LANES))


def _expert_kernel(idx_ref, h2_hbm, wg_ref, wu_ref, wd_ref, x1_hbm, out_hbm, gbuf, xe, gate, acc, sem, *, cap):
    del x1_hbm
    e = pl.program_id(0)
    f = pl.program_id(1)
    unroll = 8

    def rows(issue):
        def body(jo, carry):
            for u in range(unroll):
                j = jo * unroll + u
                issue(j, idx_ref[0, 0, j])
            return carry
        lax.fori_loop(0, cap // unroll, body, 0)

    def wait_all(hbm, n_cols):
        pltpu.make_async_copy(hbm.at[pl.ds(0, cap), :], gbuf.at[:, pl.ds(0, n_cols)], sem).wait()

    @pl.when(f == 0)
    def _gather():
        rows(lambda j, t: pltpu.make_async_copy(h2_hbm.at[pl.ds(t, 1), :], gbuf.at[pl.ds(j, 1), :], sem).start())
        wait_all(h2_hbm, D_MODEL + AFF_PAD)
        xe[...] = gbuf[:, 0:D_MODEL].astype(BF16)
        aff = gbuf[:, D_MODEL:D_MODEL + AFF_PAD]
        lane = lax.broadcasted_iota(jnp.int32, aff.shape, 1)
        gate[...] = jnp.sum(jnp.where(lane == e, aff, 0.0), axis=-1, keepdims=True)
        acc[...] = jnp.zeros(acc.shape, F32)

    x = xe[...]
    g = _dot(x, wg_ref[0].astype(BF16))
    u = _dot(x, wu_ref[0].astype(BF16))
    hid = (g * (1.0 / (1.0 + jnp.exp(-g))) * u).astype(BF16)
    acc[...] += _dot(hid, wd_ref[0].astype(BF16))

    @pl.when(f == pl.num_programs(1) - 1)
    def _scatter_add():
        cols = pl.ds(0, D_MODEL)
        rows(lambda j, t: pltpu.make_async_copy(out_hbm.at[pl.ds(t, 1), :], gbuf.at[pl.ds(j, 1), cols], sem).start())
        wait_all(out_hbm, D_MODEL)
        gbuf[:, 0:D_MODEL] = gbuf[:, 0:D_MODEL] + acc[...] * gate[...]
        rows(lambda j, t: pltpu.make_async_copy(gbuf.at[pl.ds(j, 1), cols], out_hbm.at[pl.ds(t, 1), :], sem).start())
        wait_all(out_hbm, D_MODEL)


def _experts(idx, h2_ext, x1, w_gate, w_up, w_down, cap):
    n = x1.shape[0]
    nf = EXPERT_FF // FF_TILE
    return pl.pallas_call(
        functools.partial(_expert_kernel, cap=cap),
        grid=(N_EXPERTS, nf),
        in_specs=[pl.BlockSpec((1, 1, cap), lambda e, f: (e, 0, 0), memory_space=pltpu.SMEM),
                  pl.BlockSpec(memory_space=pl.ANY),
                  pl.BlockSpec((1, D_MODEL, FF_TILE), lambda e, f: (e, 0, f)),
                  pl.BlockSpec((1, D_MODEL, FF_TILE), lambda e, f: (e, 0, f)),
                  pl.BlockSpec((1, FF_TILE, D_MODEL), lambda e, f: (e, f, 0)),
                  pl.BlockSpec(memory_space=pl.ANY)],
        out_specs=pl.BlockSpec(memory_space=pl.ANY),
        out_shape=jax.ShapeDtypeStruct((n, D_MODEL), F32),
        scratch_shapes=[pltpu.VMEM((cap, D_MODEL + AFF_PAD), F32), pltpu.VMEM((cap, D_MODEL), BF16),
                        pltpu.VMEM((cap, 1), F32), pltpu.VMEM((cap, D_MODEL), F32), pltpu.SemaphoreType.DMA(())],
        input_output_aliases={5: 0},
        name="experts",
        compiler_params=_cparams(("arbitrary", "arbitrary")),
    )(idx.reshape(N_EXPERTS, 1, cap), h2_ext, w_gate, w_up, w_down, x1)


def _final_kernel(x_ref, g_ref, o_ref):
    o_ref[...] = _rms(x_ref[...], g_ref[...])


def _final_norm(x2d, gain):
    n = x2d.shape[0]
    tm = ROW_TILE
    return pl.pallas_call(
        _final_kernel,
        grid=(n // tm,),
        in_specs=[pl.BlockSpec((tm, D_MODEL), lambda i: (i, 0)), pl.BlockSpec((1, D_MODEL), lambda i: (0, 0))],
        out_specs=pl.BlockSpec((tm, D_MODEL), lambda i: (i, 0)),
        out_shape=jax.ShapeDtypeStruct((n, D_MODEL), F32),
        name="final_norm",
        compiler_params=_cparams(("parallel",)),
    )(x2d, gain.reshape(1, D_MODEL))


def _trunk(x, norm_mix, w_in, q_norm_b, k_norm_b, norm_out_a, norm_out_b, w_out, norm_ffn, w_router,
           w_gate, w_up, w_down, norm_final):
    batch, seq, _ = x.shape
    n = batch * seq
    cap = EC_CAPACITY_FACTOR * n // N_EXPERTS
    x2d = x.reshape(n, D_MODEL)
    tables = _rope_tables(seq)
    for l in range(norm_mix.shape[0]):
        qa, ka, va, qb, kb, vb = _project(x2d, seq, norm_mix[l], w_in[l].astype(BF16), q_norm_b[l], k_norm_b[l],
                                          tables)
        pats = [_banded_attention(qa, ka, va, batch, seq, d) for d in DILATIONS]
        ob = _gqa_attention(qb, kb, vb, batch, seq)
        x1, h2_ext, aff_t = _post_attention([p[0] for p in pats], [p[1] for p in pats], ob, x2d, norm_out_a[l],
                                            norm_out_b[l], w_out[l].astype(BF16), norm_ffn[l],
                                            w_router[l].astype(BF16))
        idx = _select(aff_t, cap)
        x2d = _experts(idx, h2_ext, x1, w_gate[l], w_up[l], w_down[l], cap)
    return _final_norm(x2d, norm_final).reshape(batch, seq, D_MODEL)


def kernel(x_prompt, x_sample, norm_mix, w_in, q_norm_b, k_norm_b, norm_out_a, norm_out_b, w_out, norm_ffn,
           w_router, w_gate, w_up, w_down, norm_final):
    params = (norm_mix, w_in, q_norm_b, k_norm_b, norm_out_a, norm_out_b, w_out, norm_ffn, w_router, w_gate,
              w_up, w_down, norm_final)
    return (_trunk(x_prompt, *params), _trunk(x_sample, *params))
```

```python
import functools

import jax
import jax.numpy as jnp
from jax import lax
from jax.experimental import pallas as pl
from jax.experimental.pallas import tpu as pltpu

F32 = jnp.float32
BF16 = jnp.bfloat16

D_MODEL = 1024
HEAD_DIM = 64
N_HEADS_A = 8
N_HEADS_B = 8
N_KV_B = 2
WIDTH_A = N_HEADS_A * HEAD_DIM
WIDTH_B = N_HEADS_B * HEAD_DIM
KV_WIDTH_B = N_KV_B * HEAD_DIM
IN_WIDTH = 3 * WIDTH_A + WIDTH_B + 2 * KV_WIDTH_B
DILATIONS = (1, 4, 16)
BAND_HALF = 64
ROPE_THETA_A = 500000.0
ROT_DIM_A = HEAD_DIM // 4
ROPE_THETA_B = 10000.0
GRID_W = 64
N_EXPERTS = 16
EC_CAPACITY_FACTOR = 2
EXPERT_FF = 2816
EPS = 1e-6
NEG_INF = -1e30
SCALE = HEAD_DIM ** -0.5
LOG2E = 1.4426950408889634

LANES = 128
ROW_TILE = 256
Q_TILE = 128
QB_TILE = 128
KB_TILE = 1024
FF_TILE = 256
AFF_PAD = LANES
VMEM_LIMIT = 56 * 1024 * 1024


def _cparams(sem):
    return pltpu.CompilerParams(dimension_semantics=sem, vmem_limit_bytes=VMEM_LIMIT)


def _rms(x, g):
    return x * lax.rsqrt(jnp.mean(x * x, axis=-1, keepdims=True) + EPS) * g


def _dot(a, b):
    return jnp.dot(a, b, preferred_element_type=F32)


def _dot_nt(a, b):
    return lax.dot_general(a, b, (((1,), (1,)), ((), ())), preferred_element_type=F32)


def _rope(v, c, s_next, s_prev, shift):
    return v * c + pltpu.roll(v, LANES - shift, 1) * s_next + pltpu.roll(v, shift, 1) * s_prev


def _head_norm(v, gain, ones_bd):
    sq = v * v
    hi = sq.astype(BF16)
    lo = (sq - hi.astype(F32)).astype(BF16)
    ss = _dot(hi, ones_bd) + _dot(lo, ones_bd)
    return v * lax.rsqrt(ss * (1.0 / HEAD_DIM) + EPS) * gain


def _proj_kernel(x_ref, g_ref, w_ref, qn_ref, kn_ref, ca_ref, sa1_ref, sa2_ref, cb_ref, sb1_ref, sb2_ref,
                 ones_ref, qa_ref, ka_ref, va_ref, qb_ref, kb_ref, vb_ref):
    h = _rms(x_ref[...], g_ref[...]).astype(BF16)
    ca, sa1, sa2 = ca_ref[...], sa1_ref[...], sa2_ref[...]
    cb, sb1, sb2 = cb_ref[...], sb1_ref[...], sb2_ref[...]
    ones_bd = ones_ref[...]
    o_ka, o_va, o_qb, o_kb, o_vb = WIDTH_A, 2 * WIDTH_A, 3 * WIDTH_A, 3 * WIDTH_A + WIDTH_B, 3 * WIDTH_A + WIDTH_B + KV_WIDTH_B

    p = _dot(h, w_ref[:, 0:o_ka])
    for c in range(WIDTH_A // LANES):
        sl = slice(c * LANES, (c + 1) * LANES)
        qa_ref[:, sl] = (_rope(p[:, sl], ca, sa1, sa2, ROT_DIM_A // 2) * SCALE).astype(BF16)
    p = _dot(h, w_ref[:, o_ka:o_va])
    for c in range(WIDTH_A // LANES):
        sl = slice(c * LANES, (c + 1) * LANES)
        ka_ref[:, sl] = _rope(p[:, sl], ca, sa1, sa2, ROT_DIM_A // 2).astype(BF16)
    va_ref[...] = _dot(h, w_ref[:, o_va:o_qb]).astype(BF16)

    p = _dot(h, w_ref[:, o_qb:o_kb])
    qn = qn_ref[...]
    for c in range(WIDTH_B // LANES):
        sl = slice(c * LANES, (c + 1) * LANES)
        v = _head_norm(p[:, sl], qn, ones_bd)
        qb_ref[:, sl] = (_rope(v, cb, sb1, sb2, HEAD_DIM // 4) * (SCALE * LOG2E)).astype(BF16)
    p = _dot(h, w_ref[:, o_kb:o_vb])
    v = _head_norm(p, kn_ref[...], ones_bd)
    kb_ref[...] = _rope(v, cb, sb1, sb2, HEAD_DIM // 4).astype(BF16)
    vb = _dot(h, w_ref[:, o_vb:IN_WIDTH])
    lo_half = lax.broadcasted_iota(jnp.int32, vb.shape, 1) < HEAD_DIM
    vb_ref[:, 0:LANES] = jnp.where(lo_half, vb, 1.0).astype(BF16)
    vb_ref[:, LANES:2 * LANES] = jnp.where(lo_half, 1.0, vb).astype(BF16)


def _rope_tables(seq):
    lane = jnp.arange(LANES) % HEAD_DIM
    pos = jnp.arange(seq)

    def cos_sin(p, dim, theta):
        inv = theta ** (-jnp.arange(0, dim, 2, dtype=F32) / dim)
        ang = p.astype(F32)[:, None] * inv[None, :]
        return jnp.cos(ang), jnp.sin(ang)

    ha = ROT_DIM_A // 2
    cos_a, sin_a = cos_sin(pos, ROT_DIM_A, ROPE_THETA_A)
    fa = lane % ha
    in_lo, in_hi = lane < ha, (lane >= ha) & (lane < ROT_DIM_A)
    ca = jnp.where((in_lo | in_hi)[None, :], cos_a[:, fa], 1.0)
    sa1 = jnp.where(in_lo[None, :], -sin_a[:, fa], 0.0)
    sa2 = jnp.where(in_hi[None, :], sin_a[:, fa], 0.0)
    hb = HEAD_DIM // 4
    cos_r, sin_r = cos_sin(pos // GRID_W, HEAD_DIM // 2, ROPE_THETA_B)
    cos_c, sin_c = cos_sin(pos % GRID_W, HEAD_DIM // 2, ROPE_THETA_B)
    fb = lane % hb
    is_row = (lane < HEAD_DIM // 2)[None, :]
    cosb = jnp.where(is_row, cos_r[:, fb], cos_c[:, fb])
    sinb = jnp.where(is_row, sin_r[:, fb], sin_c[:, fb])
    first = ((lane // hb) % 2 == 0)[None, :]
    sb1 = jnp.where(first, -sinb, 0.0)
    sb2 = jnp.where(first, 0.0, sinb)
    return ca, sa1, sa2, cosb, sb1, sb2


def _project(x2d, seq, norm_mix, w_in_bf, q_norm, k_norm, tables):
    n = x2d.shape[0]
    tm = ROW_TILE
    tiles_per_seq = seq // tm
    row = lambda i: (i, 0)
    fixed = lambda i: (0, 0)
    tab = lambda i: (i % tiles_per_seq, 0)
    ones_bd = (jnp.arange(LANES)[:, None] // HEAD_DIM == jnp.arange(LANES)[None, :] // HEAD_DIM).astype(BF16)
    gain2 = lambda g: jnp.tile(g.reshape(1, HEAD_DIM), (1, LANES // HEAD_DIM))
    tab_spec = pl.BlockSpec((tm, LANES), tab)
    out_shapes = (
        jax.ShapeDtypeStruct((n, WIDTH_A), BF16), jax.ShapeDtypeStruct((n, WIDTH_A), BF16),
        jax.ShapeDtypeStruct((n, WIDTH_A), BF16), jax.ShapeDtypeStruct((n, WIDTH_B), BF16),
        jax.ShapeDtypeStruct((n, KV_WIDTH_B), BF16), jax.ShapeDtypeStruct((n, 2 * KV_WIDTH_B), BF16))
    return pl.pallas_call(
        _proj_kernel,
        grid=(n // tm,),
        in_specs=[pl.BlockSpec((tm, D_MODEL), row), pl.BlockSpec((1, D_MODEL), fixed),
                  pl.BlockSpec((D_MODEL, IN_WIDTH), fixed), pl.BlockSpec((1, LANES), fixed),
                  pl.BlockSpec((1, LANES), fixed)] + [tab_spec] * 6 + [pl.BlockSpec((LANES, LANES), fixed)],
        out_specs=[pl.BlockSpec((tm, WIDTH_A), row)] * 3 + [pl.BlockSpec((tm, WIDTH_B), row)]
        + [pl.BlockSpec((tm, KV_WIDTH_B), row), pl.BlockSpec((tm, 2 * KV_WIDTH_B), row)],
        out_shape=out_shapes,
        name="in_proj",
        compiler_params=_cparams(("parallel",)),
    )(x2d, norm_mix.reshape(1, D_MODEL), w_in_bf, gain2(q_norm), gain2(k_norm), *tables, ones_bd)


def _band_kernel(q_ref, kp_ref, kc_ref, kn_ref, vp_ref, vc_ref, vn_ref, o_ref, lse_ref, *, sub_len):
    i = pl.program_id(2)
    t = Q_TILE
    q = q_ref[0]
    kw = jnp.concatenate([kp_ref[0, t - BAND_HALF:t], kc_ref[0], kn_ref[0, 0:BAND_HALF]], axis=0)
    vw = jnp.concatenate([vp_ref[0, t - BAND_HALF:t], vc_ref[0], vn_ref[0, 0:BAND_HALF]], axis=0)
    nk = t + 2 * BAND_HALF
    row = lax.broadcasted_iota(jnp.int32, (t, nk), 0)
    col = lax.broadcasted_iota(jnp.int32, (t, nk), 1)
    kj = i * t - BAND_HALF + col
    rel = col - BAND_HALF - row
    valid = (kj >= 0) & (kj < sub_len) & (rel >= -BAND_HALF) & (rel <= BAND_HALF)
    lo_half = lax.broadcasted_iota(jnp.int32, (t, LANES), 1) < HEAD_DIM
    zero = jnp.zeros((), BF16)
    for c in range(WIDTH_A // LANES):
        sl = slice(c * LANES, (c + 1) * LANES)
        qc, kc, vc = q[:, sl], kw[:, sl], vw[:, sl]
        outs, lses = [], []
        for half_mask in (lo_half, jnp.logical_not(lo_half)):
            s = _dot_nt(jnp.where(half_mask, qc, zero), kc)
            s = jnp.where(valid, s, NEG_INF)
            m = jnp.max(s, axis=-1, keepdims=True)
            p = jnp.exp(s - m)
            l = jnp.sum(p, axis=-1, keepdims=True)
            outs.append(_dot(p.astype(BF16), vc) / l)
            lses.append(jnp.broadcast_to(m + jnp.log(l), (t, LANES)))
        o_ref[0, :, sl] = jnp.where(lo_half, outs[0], outs[1])
        lse_ref[0, :, sl] = jnp.where(lo_half, lses[0], lses[1])


def _banded_attention(q, k, v, batch, seq, dil):
    sub_len = seq // dil
    nb = sub_len // Q_TILE
    view = lambda a: a.reshape(batch, sub_len, dil * WIDTH_A)
    blk = (1, Q_TILE, WIDTH_A)
    cur = pl.BlockSpec(blk, lambda b, r, i: (b, i, r))
    prev = pl.BlockSpec(blk, lambda b, r, i: (b, jnp.maximum(i - 1, 0), r))
    nxt = pl.BlockSpec(blk, lambda b, r, i: (b, jnp.minimum(i + 1, nb - 1), r))
    shape = jax.ShapeDtypeStruct((batch, sub_len, dil * WIDTH_A), F32)
    o, lse = pl.pallas_call(
        functools.partial(_band_kernel, sub_len=sub_len),
        grid=(batch, dil, nb),
        in_specs=[cur, prev, cur, nxt, prev, cur, nxt],
        out_specs=[cur, cur],
        out_shape=(shape, shape),
        name=f"band_attn_d{dil}",
        compiler_params=_cparams(("parallel", "parallel", "parallel")),
    )(view(q), view(k), view(k), view(k), view(v), view(v), view(v))
    return o.reshape(batch * seq, WIDTH_A), lse.reshape(batch * seq, WIDTH_A)


def _gqa_kernel(q_ref, k_ref, v_ref, o_ref, qs_ref, m_ref, acc_ref, *, seq):
    tq = QB_TILE
    group = N_HEADS_B // N_KV_B
    lane_half = lax.broadcasted_iota(jnp.int32, (tq, LANES), 1) // HEAD_DIM
    heads_out = [None] * N_HEADS_B
    for n in range(N_KV_B):
        for g in range(group):
            hd = n * group + g
            qc = q_ref[0, :, (hd // 2) * LANES:(hd // 2 + 1) * LANES].astype(F32)
            if hd % 2 != n:
                qc = pltpu.roll(qc, HEAD_DIM, 1)
            qs_ref[g * tq:(g + 1) * tq, :] = jnp.where(lane_half == n, qc, 0.0).astype(BF16)
        m_ref[...] = jnp.full(m_ref.shape, -jnp.inf, F32)
        acc_ref[...] = jnp.zeros(acc_ref.shape, F32)

        def kv_step(j, carry):
            start = pl.multiple_of(j * KB_TILE, KB_TILE)
            kc = k_ref[0, pl.ds(start, KB_TILE), :]
            vc = v_ref[0, pl.ds(start, KB_TILE), n * LANES:(n + 1) * LANES]
            s = _dot_nt(qs_ref[...], kc)
            m_old = m_ref[...]
            m_new = jnp.maximum(m_old, jnp.max(s, axis=-1, keepdims=True))
            p = jnp.exp2(s - pltpu.repeat(m_new, KB_TILE // LANES, axis=1))
            acc_ref[...] = jnp.exp2(m_old - m_new) * acc_ref[...] + _dot(p.astype(BF16), vc)
            m_ref[...] = m_new
            return carry

        lax.fori_loop(0, seq // KB_TILE, kv_step, 0)
        acc = acc_ref[...]
        res = acc / pltpu.roll(acc, HEAD_DIM, 1)
        for g in range(group):
            heads_out[n * group + g] = res[g * tq:(g + 1) * tq, :]
    for c in range(WIDTH_B // LANES):
        even, odd = heads_out[2 * c], heads_out[2 * c + 1]
        n = (2 * c) // group
        if n == 0:
            odd = pltpu.roll(odd, HEAD_DIM, 1)
        else:
            even = pltpu.roll(even, HEAD_DIM, 1)
        o_ref[0, :, c * LANES:(c + 1) * LANES] = jnp.where(lane_half == 0, even, odd)


def _gqa_attention(q, k, v, batch, seq):
    tq = QB_TILE
    rows = (N_HEADS_B // N_KV_B) * tq
    o = pl.pallas_call(
        functools.partial(_gqa_kernel, seq=seq),
        grid=(batch, seq // tq),
        in_specs=[pl.BlockSpec((1, tq, WIDTH_B), lambda b, i: (b, i, 0)),
                  pl.BlockSpec((1, seq, KV_WIDTH_B), lambda b, i: (b, 0, 0)),
                  pl.BlockSpec((1, seq, 2 * KV_WIDTH_B), lambda b, i: (b, 0, 0))],
        out_specs=pl.BlockSpec((1, tq, WIDTH_B), lambda b, i: (b, i, 0)),
        out_shape=jax.ShapeDtypeStruct((batch, seq, WIDTH_B), F32),
        scratch_shapes=[pltpu.VMEM((rows, LANES), BF16), pltpu.VMEM((rows, LANES), F32),
                        pltpu.VMEM((rows, LANES), F32)],
        name="gqa_attn",
        compiler_params=_cparams(("parallel", "parallel")),
    )(q.reshape(batch, seq, WIDTH_B), k.reshape(batch, seq, KV_WIDTH_B), v.reshape(batch, seq, 2 * KV_WIDTH_B))
    return o.reshape(batch * seq, WIDTH_B)


def _post_kernel(o1_ref, o2_ref, o3_ref, l1_ref, l2_ref, l3_ref, ob_ref, x_ref, ga_ref, gb_ref, wo_ref,
                 gf_ref, wr_ref, x1_ref, h2_ref, afft_ref):
    l1, l2, l3 = l1_ref[...], l2_ref[...], l3_ref[...]
    mx = jnp.maximum(jnp.maximum(l1, l2), l3)
    e1, e2, e3 = jnp.exp(l1 - mx), jnp.exp(l2 - mx), jnp.exp(l3 - mx)
    oa = (e1 * o1_ref[...] + e2 * o2_ref[...] + e3 * o3_ref[...]) / (e1 + e2 + e3)
    na = _rms(oa, ga_ref[...]).astype(BF16)
    nb = _rms(ob_ref[...], gb_ref[...]).astype(BF16)
    mixed = _dot(na, wo_ref[0:WIDTH_A, :]) + _dot(nb, wo_ref[WIDTH_A:WIDTH_A + WIDTH_B, :])
    x1 = x_ref[...] + mixed
    x1_ref[...] = x1
    h2 = _rms(x1, gf_ref[...])
    logits = _dot(h2.astype(BF16), wr_ref[...])
    lane = lax.broadcasted_iota(jnp.int32, logits.shape, 1)
    logits = jnp.where(lane < N_EXPERTS, logits, -jnp.inf)
    ex = jnp.exp(logits - jnp.max(logits, axis=-1, keepdims=True))
    aff = ex / jnp.sum(ex, axis=-1, keepdims=True)
    h2_ref[:, 0:D_MODEL] = h2
    h2_ref[:, D_MODEL:D_MODEL + AFF_PAD] = aff
    afft_ref[...] = aff.T[0:N_EXPERTS, :]


def _post_attention(oas, lses, ob, x2d, norm_out_a, norm_out_b, w_out_bf, norm_ffn, w_router_bf):
    n = x2d.shape[0]
    tm = ROW_TILE
    row = lambda i: (i, 0)
    fixed = lambda i: (0, 0)
    half = pl.BlockSpec((tm, WIDTH_A), row)
    wr_pad = jnp.zeros((D_MODEL, AFF_PAD), BF16).at[:, :N_EXPERTS].set(w_router_bf)
    return pl.pallas_call(
        _post_kernel,
        grid=(n // tm,),
        in_specs=[half] * 7 + [pl.BlockSpec((tm, D_MODEL), row), pl.BlockSpec((1, WIDTH_A), fixed),
                               pl.BlockSpec((1, WIDTH_B), fixed), pl.BlockSpec((WIDTH_A + WIDTH_B, D_MODEL), fixed),
                               pl.BlockSpec((1, D_MODEL), fixed), pl.BlockSpec((D_MODEL, AFF_PAD), fixed)],
        out_specs=[pl.BlockSpec((tm, D_MODEL), row), pl.BlockSpec((tm, D_MODEL + AFF_PAD), row),
                   pl.BlockSpec((N_EXPERTS, tm), lambda i: (0, i))],
        out_shape=(jax.ShapeDtypeStruct((n, D_MODEL), F32), jax.ShapeDtypeStruct((n, D_MODEL + AFF_PAD), F32),
                   jax.ShapeDtypeStruct((N_EXPERTS, n), F32)),
        name="post_attn",
        compiler_params=_cparams(("parallel",)),
    )(*oas, *lses, ob, x2d, norm_out_a.reshape(1, WIDTH_A), norm_out_b.reshape(1, WIDTH_B), w_out_bf,
      norm_ffn.reshape(1, D_MODEL), wr_pad)


def _select_kernel(aff_ref, idx_ref, *, cap):
    ne, nch, _ = aff_ref.shape
    bits = pltpu.bitcast(aff_ref[...], jnp.int32)

    def count(mask):
        return jnp.sum(jnp.sum(mask.astype(F32), axis=1, keepdims=True), axis=2, keepdims=True)

    def search(i, thr):
        cand = thr | jnp.left_shift(jnp.int32(1), 30 - i)
        return jnp.where(count(bits >= cand) >= cap, cand, thr)

    thr = lax.fori_loop(0, 31, search, jnp.zeros((ne, 1, 1), jnp.int32))

    ci = lax.broadcasted_iota(jnp.int32, (LANES, LANES), 0)
    cj = lax.broadcasted_iota(jnp.int32, (LANES, LANES), 1)
    upper = (ci <= cj).astype(BF16)
    ones = jnp.ones((LANES, LANES), BF16)
    ri = lax.broadcasted_iota(jnp.int32, (nch, nch), 0)
    rj = lax.broadcasted_iota(jnp.int32, (nch, nch), 1)
    strict_lower = (rj < ri).astype(BF16)

    def prefix(mask_f):
        mb = mask_f.astype(BF16).reshape(ne * nch, LANES)
        local = _dot(mb, upper).reshape(ne, nch, LANES)
        tot = _dot(mb, ones).reshape(ne, nch, LANES)
        offs = jnp.stack([_dot(strict_lower, tot[e].astype(BF16)) for e in range(ne)])
        return local, tot, offs

    gt = bits > thr
    eq = bits == thr
    need = cap - count(gt)
    eq_local, _, eq_offs = prefix(eq.astype(F32))
    sel = jnp.logical_or(gt, jnp.logical_and(eq, (eq_local + eq_offs) <= need)).astype(F32)
    loc, tot, offs = prefix(sel)

    slot = lax.broadcasted_iota(jnp.int32, (1, cap), 1).astype(F32)
    chunk_id = lax.broadcasted_iota(jnp.int32, (nch, 1), 0).astype(F32)
    for e in range(ne):
        before = offs[e][:, 0:1]
        through = before + tot[e][:, 0:1]
        chunk_of = jnp.sum((through <= slot).astype(F32), axis=0, keepdims=True)
        onehot = chunk_id == chunk_of
        base = jnp.sum(jnp.where(onehot, before, 0.0), axis=0, keepdims=True)
        rows = _dot(loc[e].T.astype(BF16), onehot.astype(BF16))
        lane_of = jnp.sum((rows <= slot - base).astype(F32), axis=0, keepdims=True)
        idx_ref[e:e + 1, :] = (chunk_of * LANES + lane_of).astype(jnp.int32)


def _select(aff_t, cap):
    ne, n = aff_t.shape
    return pl.pallas_call(
        functools.partial(_select_kernel, cap=cap),
        out_shape=jax.ShapeDtypeStruct((ne, cap), jnp.int32),
        name="ec_select",
        compiler_params=pltpu.CompilerParams(vmem_limit_bytes=VMEM_LIMIT),
    )(aff_t.reshape(ne, n // LANES, LANES))


def _expert_kernel(idx_ref, h2_hbm, wg_ref, wu_ref, wd_ref, x1_hbm, out_hbm, gbuf, xe, gate, acc, sem, *, cap):
    del x1_hbm
    e = pl.program_id(0)
    f = pl.program_id(1)
    unroll = 8

    def rows(issue):
        def body(jo, carry):
            for u in range(unroll):
                j = jo * unroll + u
                issue(j, idx_ref[0, 0, j])
            return carry
        lax.fori_loop(0, cap // unroll, body, 0)

    def wait_all(hbm, n_cols):
        pltpu.make_async_copy(hbm.at[pl.ds(0, cap), :], gbuf.at[:, pl.ds(0, n_cols)], sem).wait()

    @pl.when(f == 0)
    def _gather():
        rows(lambda j, t: pltpu.make_async_copy(h2_hbm.at[pl.ds(t, 1), :], gbuf.at[pl.ds(j, 1), :], sem).start())
        wait_all(h2_hbm, D_MODEL + AFF_PAD)
        xe[...] = gbuf[:, 0:D_MODEL].astype(BF16)
        aff = gbuf[:, D_MODEL:D_MODEL + AFF_PAD]
        lane = lax.broadcasted_iota(jnp.int32, aff.shape, 1)
        gate[...] = jnp.sum(jnp.where(lane == e, aff, 0.0), axis=-1, keepdims=True)
        acc[...] = jnp.zeros(acc.shape, F32)

    x = xe[...]
    g = _dot(x, wg_ref[0].astype(BF16))
    u = _dot(x, wu_ref[0].astype(BF16))
    hid = (g * (1.0 / (1.0 + jnp.exp(-g))) * u).astype(BF16)
    acc[...] += _dot(hid, wd_ref[0].astype(BF16))

    @pl.when(f == pl.num_programs(1) - 1)
    def _scatter_add():
        cols = pl.ds(0, D_MODEL)
        rows(lambda j, t: pltpu.make_async_copy(out_hbm.at[pl.ds(t, 1), :], gbuf.at[pl.ds(j, 1), cols], sem).start())
        wait_all(out_hbm, D_MODEL)
        gbuf[:, 0:D_MODEL] = gbuf[:, 0:D_MODEL] + acc[...] * gate[...]
        rows(lambda j, t: pltpu.make_async_copy(gbuf.at[pl.ds(j, 1), cols], out_hbm.at[pl.ds(t, 1), :], sem).start())
        wait_all(out_hbm, D_MODEL)


def _experts(idx, h2_ext, x1, w_gate, w_up, w_down, cap):
    n = x1.shape[0]
    nf = EXPERT_FF // FF_TILE
    return pl.pallas_call(
        functools.partial(_expert_kernel, cap=cap),
        grid=(N_EXPERTS, nf),
        in_specs=[pl.BlockSpec((1, 1, cap), lambda e, f: (e, 0, 0), memory_space=pltpu.SMEM),
                  pl.BlockSpec(memory_space=pl.ANY),
                  pl.BlockSpec((1, D_MODEL, FF_TILE), lambda e, f: (e, 0, f)),
                  pl.BlockSpec((1, D_MODEL, FF_TILE), lambda e, f: (e, 0, f)),
                  pl.BlockSpec((1, FF_TILE, D_MODEL), lambda e, f: (e, f, 0)),
                  pl.BlockSpec(memory_space=pl.ANY)],
        out_specs=pl.BlockSpec(memory_space=pl.ANY),
        out_shape=jax.ShapeDtypeStruct((n, D_MODEL), F32),
        scratch_shapes=[pltpu.VMEM((cap, D_MODEL + AFF_PAD), F32), pltpu.VMEM((cap, D_MODEL), BF16),
                        pltpu.VMEM((cap, 1), F32), pltpu.VMEM((cap, D_MODEL), F32), pltpu.SemaphoreType.DMA(())],
        input_output_aliases={5: 0},
        name="experts",
        compiler_params=_cparams(("arbitrary", "arbitrary")),
    )(idx.reshape(N_EXPERTS, 1, cap), h2_ext, w_gate, w_up, w_down, x1)


def _final_kernel(x_ref, g_ref, o_ref):
    o_ref[...] = _rms(x_ref[...], g_ref[...])


def _final_norm(x2d, gain):
    n = x2d.shape[0]
    tm = ROW_TILE
    return pl.pallas_call(
        _final_kernel,
        grid=(n // tm,),
        in_specs=[pl.BlockSpec((tm, D_MODEL), lambda i: (i, 0)), pl.BlockSpec((1, D_MODEL), lambda i: (0, 0))],
        out_specs=pl.BlockSpec((tm, D_MODEL), lambda i: (i, 0)),
        out_shape=jax.ShapeDtypeStruct((n, D_MODEL), F32),
        name="final_norm",
        compiler_params=_cparams(("parallel",)),
    )(x2d, gain.reshape(1, D_MODEL))


def _trunk(x, norm_mix, w_in, q_norm_b, k_norm_b, norm_out_a, norm_out_b, w_out, norm_ffn, w_router,
           w_gate, w_up, w_down, norm_final):
    batch, seq, _ = x.shape
    n = batch * seq
    cap = EC_CAPACITY_FACTOR * n // N_EXPERTS
    x2d = x.reshape(n, D_MODEL)
    tables = _rope_tables(seq)
    for l in range(norm_mix.shape[0]):
        qa, ka, va, qb, kb, vb = _project(x2d, seq, norm_mix[l], w_in[l].astype(BF16), q_norm_b[l], k_norm_b[l],
                                          tables)
        pats = [_banded_attention(qa, ka, va, batch, seq, d) for d in DILATIONS]
        ob = _gqa_attention(qb, kb, vb, batch, seq)
        x1, h2_ext, aff_t = _post_attention([p[0] for p in pats], [p[1] for p in pats], ob, x2d, norm_out_a[l],
                                            norm_out_b[l], w_out[l].astype(BF16), norm_ffn[l],
                                            w_router[l].astype(BF16))
        idx = _select(aff_t, cap)
        x2d = _experts(idx, h2_ext, x1, w_gate[l], w_up[l], w_down[l], cap)
    return _final_norm(x2d, norm_final).reshape(batch, seq, D_MODEL)


def kernel(x_prompt, x_sample, norm_mix, w_in, q_norm_b, k_norm_b, norm_out_a, norm_out_b, w_out, norm_ffn,
           w_router, w_gate, w_up, w_down, norm_final):
    params = (norm_mix, w_in, q_norm_b, k_norm_b, norm_out_a, norm_out_b, w_out, norm_ffn, w_router, w_gate,
              w_up, w_down, norm_final)
    return (_trunk(x_prompt, *params), _trunk(x_sample, *params))
```

```python
import functools

import jax
import jax.numpy as jnp
from jax import lax
from jax.experimental import pallas as pl
from jax.experimental.pallas import tpu as pltpu

F32 = jnp.float32
BF16 = jnp.bfloat16

D_MODEL = 1024
HEAD_DIM = 64
N_HEADS_A = 8
N_HEADS_B = 8
N_KV_B = 2
WIDTH_A = N_HEADS_A * HEAD_DIM
WIDTH_B = N_HEADS_B * HEAD_DIM
KV_WIDTH_B = N_KV_B * HEAD_DIM
IN_WIDTH = 3 * WIDTH_A + WIDTH_B + 2 * KV_WIDTH_B
DILATIONS = (1, 4, 16)
BAND_HALF = 64
ROPE_THETA_A = 500000.0
ROT_DIM_A = HEAD_DIM // 4
ROPE_THETA_B = 10000.0
GRID_W = 64
N_EXPERTS = 16
EC_CAPACITY_FACTOR = 2
EXPERT_FF = 2816
EPS = 1e-6
NEG_INF = -1e30
SCALE = HEAD_DIM ** -0.5
LOG2E = 1.4426950408889634

LANES = 128
ROW_TILE = 256
Q_TILE = 128
QB_TILE = 128
KB_TILE = 1024
FF_TILE = 256
AFF_PAD = LANES
VMEM_LIMIT = 56 * 1024 * 1024


def _cparams(sem):
    return pltpu.CompilerParams(dimension_semantics=sem, vmem_limit_bytes=VMEM_LIMIT)


def _rms(x, g):
    return x * lax.rsqrt(jnp.mean(x * x, axis=-1, keepdims=True) + EPS) * g


def _dot(a, b):
    return jnp.dot(a, b, preferred_element_type=F32)


def _dot_nt(a, b):
    return lax.dot_general(a, b, (((1,), (1,)), ((), ())), preferred_element_type=F32)


def _rope(v, c, s_next, s_prev, shift):
    return v * c + pltpu.roll(v, LANES - shift, 1) * s_next + pltpu.roll(v, shift, 1) * s_prev


def _head_norm(v, gain, ones_bd):
    sq = v * v
    hi = sq.astype(BF16)
    lo = (sq - hi.astype(F32)).astype(BF16)
    ss = _dot(hi, ones_bd) + _dot(lo, ones_bd)
    return v * lax.rsqrt(ss * (1.0 / HEAD_DIM) + EPS) * gain


def _proj_kernel(x_ref, g_ref, w_ref, qn_ref, kn_ref, ca_ref, sa1_ref, sa2_ref, cb_ref, sb1_ref, sb2_ref,
                 ones_ref, qa_ref, ka_ref, va_ref, qb_ref, kb_ref, vb_ref):
    h = _rms(x_ref[...], g_ref[...]).astype(BF16)
    ca, sa1, sa2 = ca_ref[...], sa1_ref[...], sa2_ref[...]
    cb, sb1, sb2 = cb_ref[...], sb1_ref[...], sb2_ref[...]
    ones_bd = ones_ref[...]
    o_ka, o_va, o_qb, o_kb, o_vb = WIDTH_A, 2 * WIDTH_A, 3 * WIDTH_A, 3 * WIDTH_A + WIDTH_B, 3 * WIDTH_A + WIDTH_B + KV_WIDTH_B

    p = _dot(h, w_ref[:, 0:o_ka])
    for c in range(WIDTH_A // LANES):
        sl = slice(c * LANES, (c + 1) * LANES)
        qa_ref[:, sl] = (_rope(p[:, sl], ca, sa1, sa2, ROT_DIM_A // 2) * SCALE).astype(BF16)
    p = _dot(h, w_ref[:, o_ka:o_va])
    for c in range(WIDTH_A // LANES):
        sl = slice(c * LANES, (c + 1) * LANES)
        ka_ref[:, sl] = _rope(p[:, sl], ca, sa1, sa2, ROT_DIM_A // 2).astype(BF16)
    va_ref[...] = _dot(h, w_ref[:, o_va:o_qb]).astype(BF16)

    p = _dot(h, w_ref[:, o_qb:o_kb])
    qn = qn_ref[...]
    for c in range(WIDTH_B // LANES):
        sl = slice(c * LANES, (c + 1) * LANES)
        v = _head_norm(p[:, sl], qn, ones_bd)
        qb_ref[:, sl] = (_rope(v, cb, sb1, sb2, HEAD_DIM // 4) * (SCALE * LOG2E)).astype(BF16)
    p = _dot(h, w_ref[:, o_kb:o_vb])
    v = _head_norm(p, kn_ref[...], ones_bd)
    kb_ref[...] = _rope(v, cb, sb1, sb2, HEAD_DIM // 4).astype(BF16)
    vb = _dot(h, w_ref[:, o_vb:IN_WIDTH])
    lo_half = lax.broadcasted_iota(jnp.int32, vb.shape, 1) < HEAD_DIM
    vb_ref[:, 0:LANES] = jnp.where(lo_half, vb, 1.0).astype(BF16)
    vb_ref[:, LANES:2 * LANES] = jnp.where(lo_half, 1.0, vb).astype(BF16)


def _rope_tables(seq):
    lane = jnp.arange(LANES) % HEAD_DIM
    pos = jnp.arange(seq)

    def cos_sin(p, dim, theta):
        inv = theta ** (-jnp.arange(0, dim, 2, dtype=F32) / dim)
        ang = p.astype(F32)[:, None] * inv[None, :]
        return jnp.cos(ang), jnp.sin(ang)

    ha = ROT_DIM_A // 2
    cos_a, sin_a = cos_sin(pos, ROT_DIM_A, ROPE_THETA_A)
    fa = lane % ha
    in_lo, in_hi = lane < ha, (lane >= ha) & (lane < ROT_DIM_A)
    ca = jnp.where((in_lo | in_hi)[None, :], cos_a[:, fa], 1.0)
    sa1 = jnp.where(in_lo[None, :], -sin_a[:, fa], 0.0)
    sa2 = jnp.where(in_hi[None, :], sin_a[:, fa], 0.0)
    hb = HEAD_DIM // 4
    cos_r, sin_r = cos_sin(pos // GRID_W, HEAD_DIM // 2, ROPE_THETA_B)
    cos_c, sin_c = cos_sin(pos % GRID_W, HEAD_DIM // 2, ROPE_THETA_B)
    fb = lane % hb
    is_row = (lane < HEAD_DIM // 2)[None, :]
    cosb = jnp.where(is_row, cos_r[:, fb], cos_c[:, fb])
    sinb = jnp.where(is_row, sin_r[:, fb], sin_c[:, fb])
    first = ((lane // hb) % 2 == 0)[None, :]
    sb1 = jnp.where(first, -sinb, 0.0)
    sb2 = jnp.where(first, 0.0, sinb)
    return ca, sa1, sa2, cosb, sb1, sb2


def _project(x2d, seq, norm_mix, w_in_bf, q_norm, k_norm, tables):
    n = x2d.shape[0]
    tm = ROW_TILE
    tiles_per_seq = seq // tm
    row = lambda i: (i, 0)
    fixed = lambda i: (0, 0)
    tab = lambda i: (i % tiles_per_seq, 0)
    ones_bd = (jnp.arange(LANES)[:, None] // HEAD_DIM == jnp.arange(LANES)[None, :] // HEAD_DIM).astype(BF16)
    gain2 = lambda g: jnp.tile(g.reshape(1, HEAD_DIM), (1, LANES // HEAD_DIM))
    tab_spec = pl.BlockSpec((tm, LANES), tab)
    out_shapes = (
        jax.ShapeDtypeStruct((n, WIDTH_A), BF16), jax.ShapeDtypeStruct((n, WIDTH_A), BF16),
        jax.ShapeDtypeStruct((n, WIDTH_A), BF16), jax.ShapeDtypeStruct((n, WIDTH_B), BF16),
        jax.ShapeDtypeStruct((n, KV_WIDTH_B), BF16), jax.ShapeDtypeStruct((n, 2 * KV_WIDTH_B), BF16))
    return pl.pallas_call(
        _proj_kernel,
        grid=(n // tm,),
        in_specs=[pl.BlockSpec((tm, D_MODEL), row), pl.BlockSpec((1, D_MODEL), fixed),
                  pl.BlockSpec((D_MODEL, IN_WIDTH), fixed), pl.BlockSpec((1, LANES), fixed),
                  pl.BlockSpec((1, LANES), fixed)] + [tab_spec] * 6 + [pl.BlockSpec((LANES, LANES), fixed)],
        out_specs=[pl.BlockSpec((tm, WIDTH_A), row)] * 3 + [pl.BlockSpec((tm, WIDTH_B), row)]
        + [pl.BlockSpec((tm, KV_WIDTH_B), row), pl.BlockSpec((tm, 2 * KV_WIDTH_B), row)],
        out_shape=out_shapes,
        name="in_proj",
        compiler_params=_cparams(("parallel",)),
    )(x2d, norm_mix.reshape(1, D_MODEL), w_in_bf, gain2(q_norm), gain2(k_norm), *tables, ones_bd)


def _band_kernel(q_ref, kp_ref, kc_ref, kn_ref, vp_ref, vc_ref, vn_ref, o_ref, lse_ref, *, sub_len):
    i = pl.program_id(2)
    t = Q_TILE
    q = q_ref[0]
    kw = jnp.concatenate([kp_ref[0, t - BAND_HALF:t], kc_ref[0], kn_ref[0, 0:BAND_HALF]], axis=0)
    vw = jnp.concatenate([vp_ref[0, t - BAND_HALF:t], vc_ref[0], vn_ref[0, 0:BAND_HALF]], axis=0)
    nk = t + 2 * BAND_HALF
    row = lax.broadcasted_iota(jnp.int32, (t, nk), 0)
    col = lax.broadcasted_iota(jnp.int32, (t, nk), 1)
    kj = i * t - BAND_HALF + col
    rel = col - BAND_HALF - row
    valid = (kj >= 0) & (kj < sub_len) & (rel >= -BAND_HALF) & (rel <= BAND_HALF)
    lo_half = lax.broadcasted_iota(jnp.int32, (t, LANES), 1) < HEAD_DIM
    zero = jnp.zeros((), BF16)
    for c in range(WIDTH_A // LANES):
        sl = slice(c * LANES, (c + 1) * LANES)
        qc, kc, vc = q[:, sl], kw[:, sl], vw[:, sl]
        outs, lses = [], []
        for half_mask in (lo_half, jnp.logical_not(lo_half)):
            s = _dot_nt(jnp.where(half_mask, qc, zero), kc)
            s = jnp.where(valid, s, NEG_INF)
            m = jnp.max(s, axis=-1, keepdims=True)
            p = jnp.exp(s - m)
            l = jnp.sum(p, axis=-1, keepdims=True)
            outs.append(_dot(p.astype(BF16), vc) / l)
            lses.append(jnp.broadcast_to(m + jnp.log(l), (t, LANES)))
        o_ref[0, :, sl] = jnp.where(lo_half, outs[0], outs[1])
        lse_ref[0, :, sl] = jnp.where(lo_half, lses[0], lses[1])


def _banded_attention(q, k, v, batch, seq, dil):
    sub_len = seq // dil
    nb = sub_len // Q_TILE
    view = lambda a: a.reshape(batch, sub_len, dil * WIDTH_A)
    blk = (1, Q_TILE, WIDTH_A)
    cur = pl.BlockSpec(blk, lambda b, r, i: (b, i, r))
    prev = pl.BlockSpec(blk, lambda b, r, i: (b, jnp.maximum(i - 1, 0), r))
    nxt = pl.BlockSpec(blk, lambda b, r, i: (b, jnp.minimum(i + 1, nb - 1), r))
    shape = jax.ShapeDtypeStruct((batch, sub_len, dil * WIDTH_A), F32)
    o, lse = pl.pallas_call(
        functools.partial(_band_kernel, sub_len=sub_len),
        grid=(batch, dil, nb),
        in_specs=[cur, prev, cur, nxt, prev, cur, nxt],
        out_specs=[cur, cur],
        out_shape=(shape, shape),
        name=f"band_attn_d{dil}",
        compiler_params=_cparams(("parallel", "parallel", "parallel")),
    )(view(q), view(k), view(k), view(k), view(v), view(v), view(v))
    return o.reshape(batch * seq, WIDTH_A), lse.reshape(batch * seq, WIDTH_A)


def _gqa_kernel(q_ref, k_ref, v_ref, o_ref, qs_ref, m_ref, acc_ref, s0_ref, s1_ref, *, seq):
    tq = QB_TILE
    group = N_HEADS_B // N_KV_B
    lane_half = lax.broadcasted_iota(jnp.int32, (tq, LANES), 1) // HEAD_DIM
    heads_out = [None] * N_HEADS_B
    for n in range(N_KV_B):
        for g in range(group):
            hd = n * group + g
            qc = q_ref[0, :, (hd // 2) * LANES:(hd // 2 + 1) * LANES].astype(F32)
            if hd % 2 != n:
                qc = pltpu.roll(qc, HEAD_DIM, 1)
            qs_ref[g * tq:(g + 1) * tq, :] = jnp.where(lane_half == n, qc, 0.0).astype(BF16)
        m_ref[...] = jnp.full(m_ref.shape, -jnp.inf, F32)
        acc_ref[...] = jnp.zeros(acc_ref.shape, F32)

        def scores(j, dst_ref):
            start = pl.multiple_of(j * KB_TILE, KB_TILE)
            dst_ref[...] = _dot_nt(qs_ref[...], k_ref[0, pl.ds(start, KB_TILE), :])

        def accumulate(j, src_ref):
            start = pl.multiple_of(j * KB_TILE, KB_TILE)
            vc = v_ref[0, pl.ds(start, KB_TILE), n * LANES:(n + 1) * LANES]
            s = src_ref[...]
            m_old = m_ref[...]
            m_new = jnp.maximum(m_old, jnp.max(s, axis=-1, keepdims=True))
            p = jnp.exp2(s - jnp.concatenate([m_new] * (KB_TILE // LANES), axis=1))
            acc_ref[...] = jnp.exp2(m_old - m_new) * acc_ref[...] + _dot(p.astype(BF16), vc)
            m_ref[...] = m_new

        def kv_pair(i, carry):
            j = 2 * i
            scores(j + 1, s1_ref)
            accumulate(j, s0_ref)
            scores(j + 2, s0_ref)
            accumulate(j + 1, s1_ref)
            return carry

        n_chunks = seq // KB_TILE
        scores(0, s0_ref)
        lax.fori_loop(0, n_chunks // 2 - 1, kv_pair, 0)
        scores(n_chunks - 1, s1_ref)
        accumulate(n_chunks - 2, s0_ref)
        accumulate(n_chunks - 1, s1_ref)
        acc = acc_ref[...]
        res = acc / pltpu.roll(acc, HEAD_DIM, 1)
        for g in range(group):
            heads_out[n * group + g] = res[g * tq:(g + 1) * tq, :]
    for c in range(WIDTH_B // LANES):
        even, odd = heads_out[2 * c], heads_out[2 * c + 1]
        n = (2 * c) // group
        if n == 0:
            odd = pltpu.roll(odd, HEAD_DIM, 1)
        else:
            even = pltpu.roll(even, HEAD_DIM, 1)
        o_ref[0, :, c * LANES:(c + 1) * LANES] = jnp.where(lane_half == 0, even, odd)


def _gqa_attention(q, k, v, batch, seq):
    tq = QB_TILE
    rows = (N_HEADS_B // N_KV_B) * tq
    o = pl.pallas_call(
        functools.partial(_gqa_kernel, seq=seq),
        grid=(batch, seq // tq),
        in_specs=[pl.BlockSpec((1, tq, WIDTH_B), lambda b, i: (b, i, 0)),
                  pl.BlockSpec((1, seq, KV_WIDTH_B), lambda b, i: (b, 0, 0)),
                  pl.BlockSpec((1, seq, 2 * KV_WIDTH_B), lambda b, i: (b, 0, 0))],
        out_specs=pl.BlockSpec((1, tq, WIDTH_B), lambda b, i: (b, i, 0)),
        out_shape=jax.ShapeDtypeStruct((batch, seq, WIDTH_B), F32),
        scratch_shapes=[pltpu.VMEM((rows, LANES), BF16), pltpu.VMEM((rows, LANES), F32),
                        pltpu.VMEM((rows, LANES), F32), pltpu.VMEM((rows, KB_TILE), F32),
                        pltpu.VMEM((rows, KB_TILE), F32)],
        name="gqa_attn",
        compiler_params=_cparams(("parallel", "parallel")),
    )(q.reshape(batch, seq, WIDTH_B), k.reshape(batch, seq, KV_WIDTH_B), v.reshape(batch, seq, 2 * KV_WIDTH_B))
    return o.reshape(batch * seq, WIDTH_B)


def _post_kernel(o1_ref, o2_ref, o3_ref, l1_ref, l2_ref, l3_ref, ob_ref, x_ref, ga_ref, gb_ref, wo_ref,
                 gf_ref, wr_ref, x1_ref, h2_ref, afft_ref):
    l1, l2, l3 = l1_ref[...], l2_ref[...], l3_ref[...]
    mx = jnp.maximum(jnp.maximum(l1, l2), l3)
    e1, e2, e3 = jnp.exp(l1 - mx), jnp.exp(l2 - mx), jnp.exp(l3 - mx)
    oa = (e1 * o1_ref[...] + e2 * o2_ref[...] + e3 * o3_ref[...]) / (e1 + e2 + e3)
    na = _rms(oa, ga_ref[...]).astype(BF16)
    nb = _rms(ob_ref[...], gb_ref[...]).astype(BF16)
    mixed = _dot(na, wo_ref[0:WIDTH_A, :]) + _dot(nb, wo_ref[WIDTH_A:WIDTH_A + WIDTH_B, :])
    x1 = x_ref[...] + mixed
    x1_ref[...] = x1
    h2 = _rms(x1, gf_ref[...])
    logits = _dot(h2.astype(BF16), wr_ref[...])
    lane = lax.broadcasted_iota(jnp.int32, logits.shape, 1)
    logits = jnp.where(lane < N_EXPERTS, logits, -jnp.inf)
    ex = jnp.exp(logits - jnp.max(logits, axis=-1, keepdims=True))
    aff = ex / jnp.sum(ex, axis=-1, keepdims=True)
    h2_ref[:, 0:D_MODEL] = h2
    h2_ref[:, D_MODEL:D_MODEL + AFF_PAD] = aff
    afft_ref[...] = aff.T[0:N_EXPERTS, :]


def _post_attention(oas, lses, ob, x2d, norm_out_a, norm_out_b, w_out_bf, norm_ffn, w_router_bf):
    n = x2d.shape[0]
    tm = ROW_TILE
    row = lambda i: (i, 0)
    fixed = lambda i: (0, 0)
    half = pl.BlockSpec((tm, WIDTH_A), row)
    wr_pad = jnp.zeros((D_MODEL, AFF_PAD), BF16).at[:, :N_EXPERTS].set(w_router_bf)
    return pl.pallas_call(
        _post_kernel,
        grid=(n // tm,),
        in_specs=[half] * 7 + [pl.BlockSpec((tm, D_MODEL), row), pl.BlockSpec((1, WIDTH_A), fixed),
                               pl.BlockSpec((1, WIDTH_B), fixed), pl.BlockSpec((WIDTH_A + WIDTH_B, D_MODEL), fixed),
                               pl.BlockSpec((1, D_MODEL), fixed), pl.BlockSpec((D_MODEL, AFF_PAD), fixed)],
        out_specs=[pl.BlockSpec((tm, D_MODEL), row), pl.BlockSpec((tm, D_MODEL + AFF_PAD), row),
                   pl.BlockSpec((N_EXPERTS, tm), lambda i: (0, i))],
        out_shape=(jax.ShapeDtypeStruct((n, D_MODEL), F32), jax.ShapeDtypeStruct((n, D_MODEL + AFF_PAD), F32),
                   jax.ShapeDtypeStruct((N_EXPERTS, n), F32)),
        name="post_attn",
        compiler_params=_cparams(("parallel",)),
    )(*oas, *lses, ob, x2d, norm_out_a.reshape(1, WIDTH_A), norm_out_b.reshape(1, WIDTH_B), w_out_bf,
      norm_ffn.reshape(1, D_MODEL), wr_pad)


def _select_kernel(aff_ref, idx_ref, *, cap):
    ne, nch, _ = aff_ref.shape
    bits = pltpu.bitcast(aff_ref[...], jnp.int32)

    def count(mask):
        return jnp.sum(jnp.sum(mask.astype(F32), axis=1, keepdims=True), axis=2, keepdims=True)

    def search(i, thr):
        cand = thr | jnp.left_shift(jnp.int32(1), 30 - i)
        return jnp.where(count(bits >= cand) >= cap, cand, thr)

    thr = lax.fori_loop(0, 31, search, jnp.zeros((ne, 1, 1), jnp.int32))

    ci = lax.broadcasted_iota(jnp.int32, (LANES, LANES), 0)
    cj = lax.broadcasted_iota(jnp.int32, (LANES, LANES), 1)
    upper = (ci <= cj).astype(BF16)
    ones = jnp.ones((LANES, LANES), BF16)
    ri = lax.broadcasted_iota(jnp.int32, (nch, nch), 0)
    rj = lax.broadcasted_iota(jnp.int32, (nch, nch), 1)
    strict_lower = (rj < ri).astype(BF16)

    def prefix(mask_f):
        mb = mask_f.astype(BF16).reshape(ne * nch, LANES)
        local = _dot(mb, upper).reshape(ne, nch, LANES)
        tot = _dot(mb, ones).reshape(ne, nch, LANES)
        offs = jnp.stack([_dot(strict_lower, tot[e].astype(BF16)) for e in range(ne)])
        return local, tot, offs

    gt = bits > thr
    eq = bits == thr
    need = cap - count(gt)
    eq_local, _, eq_offs = prefix(eq.astype(F32))
    sel = jnp.logical_or(gt, jnp.logical_and(eq, (eq_local + eq_offs) <= need)).astype(F32)
    loc, tot, offs = prefix(sel)

    slot = lax.broadcasted_iota(jnp.int32, (1, cap), 1).astype(F32)
    chunk_id = lax.broadcasted_iota(jnp.int32, (nch, 1), 0).astype(F32)
    for e in range(ne):
        before = offs[e][:, 0:1]
        through = before + tot[e][:, 0:1]
        chunk_of = jnp.sum((through <= slot).astype(F32), axis=0, keepdims=True)
        onehot = chunk_id == chunk_of
        base = jnp.sum(jnp.where(onehot, before, 0.0), axis=0, keepdims=True)
        rows = _dot(loc[e].T.astype(BF16), onehot.astype(BF16))
        lane_of = jnp.sum((rows <= slot - base).astype(F32), axis=0, keepdims=True)
        idx_ref[e:e + 1, :] = (chunk_of * LANES + lane_of).astype(jnp.int32)


def _select(aff_t, cap):
    ne, n = aff_t.shape
    return pl.pallas_call(
        functools.partial(_select_kernel, cap=cap),
        out_shape=jax.ShapeDtypeStruct((ne, cap), jnp.int32),
        name="ec_select",
        compiler_params=pltpu.CompilerParams(vmem_limit_bytes=VMEM_LIMIT),
    )(aff_t.reshape(ne, n // LANES, LANES))


def _expert_kernel(idx_ref, h2_hbm, wg_ref, wu_ref, wd_ref, x1_hbm, out_hbm, gbuf, xe, gate, acc, sem, *, cap):
    del x1_hbm
    e = pl.program_id(0)
    f = pl.program_id(1)
    unroll = 8

    def rows(issue):
        def body(jo, carry):
            for u in range(unroll):
                j = jo * unroll + u
                issue(j, idx_ref[0, 0, j])
            return carry
        lax.fori_loop(0, cap // unroll, body, 0)

    def wait_all(hbm, n_cols):
        pltpu.make_async_copy(hbm.at[pl.ds(0, cap), :], gbuf.at[:, pl.ds(0, n_cols)], sem).wait()

    @pl.when(f == 0)
    def _gather():
        rows(lambda j, t: pltpu.make_async_copy(h2_hbm.at[pl.ds(t, 1), :], gbuf.at[pl.ds(j, 1), :], sem).start())
        wait_all(h2_hbm, D_MODEL + AFF_PAD)
        xe[...] = gbuf[:, 0:D_MODEL].astype(BF16)
        aff = gbuf[:, D_MODEL:D_MODEL + AFF_PAD]
        lane = lax.broadcasted_iota(jnp.int32, aff.shape, 1)
        gate[...] = jnp.sum(jnp.where(lane == e, aff, 0.0), axis=-1, keepdims=True)
        acc[...] = jnp.zeros(acc.shape, F32)

    x = xe[...]
    g = _dot(x, wg_ref[0].astype(BF16))
    u = _dot(x, wu_ref[0].astype(BF16))
    hid = (g * (1.0 / (1.0 + jnp.exp(-g))) * u).astype(BF16)
    acc[...] += _dot(hid, wd_ref[0].astype(BF16))

    @pl.when(f == pl.num_programs(1) - 1)
    def _scatter_add():
        cols = pl.ds(0, D_MODEL)
        rows(lambda j, t: pltpu.make_async_copy(out_hbm.at[pl.ds(t, 1), :], gbuf.at[pl.ds(j, 1), cols], sem).start())
        wait_all(out_hbm, D_MODEL)
        gbuf[:, 0:D_MODEL] = gbuf[:, 0:D_MODEL] + acc[...] * gate[...]
        rows(lambda j, t: pltpu.make_async_copy(gbuf.at[pl.ds(j, 1), cols], out_hbm.at[pl.ds(t, 1), :], sem).start())
        wait_all(out_hbm, D_MODEL)


def _experts(idx, h2_ext, x1, w_gate, w_up, w_down, cap):
    n = x1.shape[0]
    nf = EXPERT_FF // FF_TILE
    return pl.pallas_call(
        functools.partial(_expert_kernel, cap=cap),
        grid=(N_EXPERTS, nf),
        in_specs=[pl.BlockSpec((1, 1, cap), lambda e, f: (e, 0, 0), memory_space=pltpu.SMEM),
                  pl.BlockSpec(memory_space=pl.ANY),
                  pl.BlockSpec((1, D_MODEL, FF_TILE), lambda e, f: (e, 0, f)),
                  pl.BlockSpec((1, D_MODEL, FF_TILE), lambda e, f: (e, 0, f)),
                  pl.BlockSpec((1, FF_TILE, D_MODEL), lambda e, f: (e, f, 0)),
                  pl.BlockSpec(memory_space=pl.ANY)],
        out_specs=pl.BlockSpec(memory_space=pl.ANY),
        out_shape=jax.ShapeDtypeStruct((n, D_MODEL), F32),
        scratch_shapes=[pltpu.VMEM((cap, D_MODEL + AFF_PAD), F32), pltpu.VMEM((cap, D_MODEL), BF16),
                        pltpu.VMEM((cap, 1), F32), pltpu.VMEM((cap, D_MODEL), F32), pltpu.SemaphoreType.DMA(())],
        input_output_aliases={5: 0},
        name="experts",
        compiler_params=_cparams(("arbitrary", "arbitrary")),
    )(idx.reshape(N_EXPERTS, 1, cap), h2_ext, w_gate, w_up, w_down, x1)


def _final_kernel(x_ref, g_ref, o_ref):
    o_ref[...] = _rms(x_ref[...], g_ref[...])


def _final_norm(x2d, gain):
    n = x2d.shape[0]
    tm = ROW_TILE
    return pl.pallas_call(
        _final_kernel,
        grid=(n // tm,),
        in_specs=[pl.BlockSpec((tm, D_MODEL), lambda i: (i, 0)), pl.BlockSpec((1, D_MODEL), lambda i: (0, 0))],
        out_specs=pl.BlockSpec((tm, D_MODEL), lambda i: (i, 0)),
        out_shape=jax.ShapeDtypeStruct((n, D_MODEL), F32),
        name="final_norm",
        compiler_params=_cparams(("parallel",)),
    )(x2d, gain.reshape(1, D_MODEL))


def _trunk(x, norm_mix, w_in, q_norm_b, k_norm_b, norm_out_a, norm_out_b, w_out, norm_ffn, w_router,
           w_gate, w_up, w_down, norm_final):
    batch, seq, _ = x.shape
    n = batch * seq
    cap = EC_CAPACITY_FACTOR * n // N_EXPERTS
    x2d = x.reshape(n, D_MODEL)
    tables = _rope_tables(seq)
    for l in range(norm_mix.shape[0]):
        qa, ka, va, qb, kb, vb = _project(x2d, seq, norm_mix[l], w_in[l].astype(BF16), q_norm_b[l], k_norm_b[l],
                                          tables)
        pats = [_banded_attention(qa, ka, va, batch, seq, d) for d in DILATIONS]
        ob = _gqa_attention(qb, kb, vb, batch, seq)
        x1, h2_ext, aff_t = _post_attention([p[0] for p in pats], [p[1] for p in pats], ob, x2d, norm_out_a[l],
                                            norm_out_b[l], w_out[l].astype(BF16), norm_ffn[l],
                                            w_router[l].astype(BF16))
        idx = _select(aff_t, cap)
        x2d = _experts(idx, h2_ext, x1, w_gate[l], w_up[l], w_down[l], cap)
    return _final_norm(x2d, norm_final).reshape(batch, seq, D_MODEL)


def kernel(x_prompt, x_sample, norm_mix, w_in, q_norm_b, k_norm_b, norm_out_a, norm_out_b, w_out, norm_ffn,
           w_router, w_gate, w_up, w_down, norm_final):
    params = (norm_mix, w_in, q_norm_b, k_norm_b, norm_out_a, norm_out_b, w_out, norm_ffn, w_router, w_gate,
              w_up, w_down, norm_final)
    return (_trunk(x_prompt, *params), _trunk(x_sample, *params))
```

```python
import functools

import jax
import jax.numpy as jnp
from jax import lax
from jax.experimental import pallas as pl
from jax.experimental.pallas import tpu as pltpu

F32 = jnp.float32
BF16 = jnp.bfloat16

D_MODEL = 1024
HEAD_DIM = 64
N_HEADS_A = 8
N_HEADS_B = 8
N_KV_B = 2
WIDTH_A = N_HEADS_A * HEAD_DIM
WIDTH_B = N_HEADS_B * HEAD_DIM
KV_WIDTH_B = N_KV_B * HEAD_DIM
IN_WIDTH = 3 * WIDTH_A + WIDTH_B + 2 * KV_WIDTH_B
DILATIONS = (1, 4, 16)
BAND_HALF = 64
ROPE_THETA_A = 500000.0
ROT_DIM_A = HEAD_DIM // 4
ROPE_THETA_B = 10000.0
GRID_W = 64
N_EXPERTS = 16
EC_CAPACITY_FACTOR = 2
EXPERT_FF = 2816
EPS = 1e-6
NEG_INF = -1e30
SCALE = HEAD_DIM ** -0.5
LOG2E = 1.4426950408889634

LANES = 128
ROW_TILE = 256
Q_TILE = 128
QB_TILE = 128
KB_TILE = 1024
FF_TILE = 256
AFF_PAD = LANES
EXPERT_ROW_CHUNK = 512
VMEM_LIMIT = 56 * 1024 * 1024
EXPERT_VMEM_LIMIT = 60 * 1024 * 1024


def _cparams(sem):
    return pltpu.CompilerParams(dimension_semantics=sem, vmem_limit_bytes=VMEM_LIMIT)


def _rms(x, g):
    return x * lax.rsqrt(jnp.mean(x * x, axis=-1, keepdims=True) + EPS) * g


def _dot(a, b):
    return jnp.dot(a, b, preferred_element_type=F32)


def _dot_nt(a, b):
    return lax.dot_general(a, b, (((1,), (1,)), ((), ())), preferred_element_type=F32)


def _rope(v, c, s_next, s_prev, shift):
    return v * c + pltpu.roll(v, LANES - shift, 1) * s_next + pltpu.roll(v, shift, 1) * s_prev


def _head_norm(v, gain, ones_bd):
    sq = v * v
    hi = sq.astype(BF16)
    lo = (sq - hi.astype(F32)).astype(BF16)
    ss = _dot(hi, ones_bd) + _dot(lo, ones_bd)
    return v * lax.rsqrt(ss * (1.0 / HEAD_DIM) + EPS) * gain


def _proj_kernel(x_ref, g_ref, w_ref, qn_ref, kn_ref, ca_ref, sa1_ref, sa2_ref, cb_ref, sb1_ref, sb2_ref,
                 ones_ref, qa_ref, ka_ref, va_ref, qb_ref, kb_ref, vb_ref):
    h = _rms(x_ref[...], g_ref[...]).astype(BF16)
    ca, sa1, sa2 = ca_ref[...], sa1_ref[...], sa2_ref[...]
    cb, sb1, sb2 = cb_ref[...], sb1_ref[...], sb2_ref[...]
    ones_bd = ones_ref[...]
    o_ka, o_va, o_qb, o_kb, o_vb = WIDTH_A, 2 * WIDTH_A, 3 * WIDTH_A, 3 * WIDTH_A + WIDTH_B, 3 * WIDTH_A + WIDTH_B + KV_WIDTH_B

    p = _dot(h, w_ref[:, 0:o_ka])
    for c in range(WIDTH_A // LANES):
        sl = slice(c * LANES, (c + 1) * LANES)
        qa_ref[:, sl] = (_rope(p[:, sl], ca, sa1, sa2, ROT_DIM_A // 2) * SCALE).astype(BF16)
    p = _dot(h, w_ref[:, o_ka:o_va])
    for c in range(WIDTH_A // LANES):
        sl = slice(c * LANES, (c + 1) * LANES)
        ka_ref[:, sl] = _rope(p[:, sl], ca, sa1, sa2, ROT_DIM_A // 2).astype(BF16)
    va_ref[...] = _dot(h, w_ref[:, o_va:o_qb]).astype(BF16)

    p = _dot(h, w_ref[:, o_qb:o_kb])
    qn = qn_ref[...]
    for c in range(WIDTH_B // LANES):
        sl = slice(c * LANES, (c + 1) * LANES)
        v = _head_norm(p[:, sl], qn, ones_bd)
        qb_ref[:, sl] = (_rope(v, cb, sb1, sb2, HEAD_DIM // 4) * (SCALE * LOG2E)).astype(BF16)
    p = _dot(h, w_ref[:, o_kb:o_vb])
    v = _head_norm(p, kn_ref[...], ones_bd)
    kb_ref[...] = _rope(v, cb, sb1, sb2, HEAD_DIM // 4).astype(BF16)
    vb = _dot(h, w_ref[:, o_vb:IN_WIDTH])
    lo_half = lax.broadcasted_iota(jnp.int32, vb.shape, 1) < HEAD_DIM
    vb_ref[:, 0:LANES] = jnp.where(lo_half, vb, 1.0).astype(BF16)
    vb_ref[:, LANES:2 * LANES] = jnp.where(lo_half, 1.0, vb).astype(BF16)


def _rope_tables(seq):
    lane = jnp.arange(LANES) % HEAD_DIM
    pos = jnp.arange(seq)

    def cos_sin(p, dim, theta):
        inv = theta ** (-jnp.arange(0, dim, 2, dtype=F32) / dim)
        ang = p.astype(F32)[:, None] * inv[None, :]
        return jnp.cos(ang), jnp.sin(ang)

    ha = ROT_DIM_A // 2
    cos_a, sin_a = cos_sin(pos, ROT_DIM_A, ROPE_THETA_A)
    fa = lane % ha
    in_lo, in_hi = lane < ha, (lane >= ha) & (lane < ROT_DIM_A)
    ca = jnp.where((in_lo | in_hi)[None, :], cos_a[:, fa], 1.0)
    sa1 = jnp.where(in_lo[None, :], -sin_a[:, fa], 0.0)
    sa2 = jnp.where(in_hi[None, :], sin_a[:, fa], 0.0)
    hb = HEAD_DIM // 4
    cos_r, sin_r = cos_sin(pos // GRID_W, HEAD_DIM // 2, ROPE_THETA_B)
    cos_c, sin_c = cos_sin(pos % GRID_W, HEAD_DIM // 2, ROPE_THETA_B)
    fb = lane % hb
    is_row = (lane < HEAD_DIM // 2)[None, :]
    cosb = jnp.where(is_row, cos_r[:, fb], cos_c[:, fb])
    sinb = jnp.where(is_row, sin_r[:, fb], sin_c[:, fb])
    first = ((lane // hb) % 2 == 0)[None, :]
    sb1 = jnp.where(first, -sinb, 0.0)
    sb2 = jnp.where(first, 0.0, sinb)
    return ca, sa1, sa2, cosb, sb1, sb2


def _project(x2d, seq, norm_mix, w_in_bf, q_norm, k_norm, tables):
    n = x2d.shape[0]
    tm = ROW_TILE
    tiles_per_seq = seq // tm
    row = lambda i: (i, 0)
    fixed = lambda i: (0, 0)
    tab = lambda i: (i % tiles_per_seq, 0)
    ones_bd = (jnp.arange(LANES)[:, None] // HEAD_DIM == jnp.arange(LANES)[None, :] // HEAD_DIM).astype(BF16)
    gain2 = lambda g: jnp.tile(g.reshape(1, HEAD_DIM), (1, LANES // HEAD_DIM))
    tab_spec = pl.BlockSpec((tm, LANES), tab)
    out_shapes = (
        jax.ShapeDtypeStruct((n, WIDTH_A), BF16), jax.ShapeDtypeStruct((n, WIDTH_A), BF16),
        jax.ShapeDtypeStruct((n, WIDTH_A), BF16), jax.ShapeDtypeStruct((n, WIDTH_B), BF16),
        jax.ShapeDtypeStruct((n, KV_WIDTH_B), BF16), jax.ShapeDtypeStruct((n, 2 * KV_WIDTH_B), BF16))
    return pl.pallas_call(
        _proj_kernel,
        grid=(n // tm,),
        in_specs=[pl.BlockSpec((tm, D_MODEL), row), pl.BlockSpec((1, D_MODEL), fixed),
                  pl.BlockSpec((D_MODEL, IN_WIDTH), fixed), pl.BlockSpec((1, LANES), fixed),
                  pl.BlockSpec((1, LANES), fixed)] + [tab_spec] * 6 + [pl.BlockSpec((LANES, LANES), fixed)],
        out_specs=[pl.BlockSpec((tm, WIDTH_A), row)] * 3 + [pl.BlockSpec((tm, WIDTH_B), row)]
        + [pl.BlockSpec((tm, KV_WIDTH_B), row), pl.BlockSpec((tm, 2 * KV_WIDTH_B), row)],
        out_shape=out_shapes,
        name="in_proj",
        compiler_params=_cparams(("parallel",)),
    )(x2d, norm_mix.reshape(1, D_MODEL), w_in_bf, gain2(q_norm), gain2(k_norm), *tables, ones_bd)


def _band_kernel(q_ref, kp_ref, kc_ref, kn_ref, vp_ref, vc_ref, vn_ref, o_ref, lse_ref, *, sub_len):
    i = pl.program_id(2)
    t = Q_TILE
    q = q_ref[0]
    kw = jnp.concatenate([kp_ref[0, t - BAND_HALF:t], kc_ref[0], kn_ref[0, 0:BAND_HALF]], axis=0)
    vw = jnp.concatenate([vp_ref[0, t - BAND_HALF:t], vc_ref[0], vn_ref[0, 0:BAND_HALF]], axis=0)
    nk = t + 2 * BAND_HALF
    row = lax.broadcasted_iota(jnp.int32, (t, nk), 0)
    col = lax.broadcasted_iota(jnp.int32, (t, nk), 1)
    kj = i * t - BAND_HALF + col
    rel = col - BAND_HALF - row
    valid = (kj >= 0) & (kj < sub_len) & (rel >= -BAND_HALF) & (rel <= BAND_HALF)
    lo_half = lax.broadcasted_iota(jnp.int32, (t, LANES), 1) < HEAD_DIM
    zero = jnp.zeros((), BF16)
    for c in range(WIDTH_A // LANES):
        sl = slice(c * LANES, (c + 1) * LANES)
        qc, kc, vc = q[:, sl], kw[:, sl], vw[:, sl]
        outs, lses = [], []
        for half_mask in (lo_half, jnp.logical_not(lo_half)):
            s = _dot_nt(jnp.where(half_mask, qc, zero), kc)
            s = jnp.where(valid, s, NEG_INF)
            m = jnp.max(s, axis=-1, keepdims=True)
            p = jnp.exp(s - m)
            l = jnp.sum(p, axis=-1, keepdims=True)
            outs.append(_dot(p.astype(BF16), vc) / l)
            lses.append(jnp.broadcast_to(m + jnp.log(l), (t, LANES)))
        o_ref[0, :, sl] = jnp.where(lo_half, outs[0], outs[1])
        lse_ref[0, :, sl] = jnp.where(lo_half, lses[0], lses[1])


def _banded_attention(q, k, v, batch, seq, dil):
    sub_len = seq // dil
    nb = sub_len // Q_TILE
    view = lambda a: a.reshape(batch, sub_len, dil * WIDTH_A)
    blk = (1, Q_TILE, WIDTH_A)
    cur = pl.BlockSpec(blk, lambda b, r, i: (b, i, r))
    prev = pl.BlockSpec(blk, lambda b, r, i: (b, jnp.maximum(i - 1, 0), r))
    nxt = pl.BlockSpec(blk, lambda b, r, i: (b, jnp.minimum(i + 1, nb - 1), r))
    shape = jax.ShapeDtypeStruct((batch, sub_len, dil * WIDTH_A), F32)
    o, lse = pl.pallas_call(
        functools.partial(_band_kernel, sub_len=sub_len),
        grid=(batch, dil, nb),
        in_specs=[cur, prev, cur, nxt, prev, cur, nxt],
        out_specs=[cur, cur],
        out_shape=(shape, shape),
        name=f"band_attn_d{dil}",
        compiler_params=_cparams(("parallel", "parallel", "parallel")),
    )(view(q), view(k), view(k), view(k), view(v), view(v), view(v))
    return o.reshape(batch * seq, WIDTH_A), lse.reshape(batch * seq, WIDTH_A)


def _gqa_kernel(q_ref, k_ref, v_ref, o_ref, qs_ref, m_ref, acc_ref, s0_ref, s1_ref, *, seq):
    tq = QB_TILE
    group = N_HEADS_B // N_KV_B
    lane_half = lax.broadcasted_iota(jnp.int32, (tq, LANES), 1) // HEAD_DIM
    heads_out = [None] * N_HEADS_B
    for n in range(N_KV_B):
        for g in range(group):
            hd = n * group + g
            qc = q_ref[0, :, (hd // 2) * LANES:(hd // 2 + 1) * LANES].astype(F32)
            if hd % 2 != n:
                qc = pltpu.roll(qc, HEAD_DIM, 1)
            qs_ref[g * tq:(g + 1) * tq, :] = jnp.where(lane_half == n, qc, 0.0).astype(BF16)
        m_ref[...] = jnp.full(m_ref.shape, -jnp.inf, F32)
        acc_ref[...] = jnp.zeros(acc_ref.shape, F32)

        def scores(j, dst_ref):
            start = pl.multiple_of(j * KB_TILE, KB_TILE)
            dst_ref[...] = _dot_nt(qs_ref[...], k_ref[0, pl.ds(start, KB_TILE), :])

        def accumulate(j, src_ref):
            start = pl.multiple_of(j * KB_TILE, KB_TILE)
            vc = v_ref[0, pl.ds(start, KB_TILE), n * LANES:(n + 1) * LANES]
            s = src_ref[...]
            m_old = m_ref[...]
            m_new = jnp.maximum(m_old, jnp.max(s, axis=-1, keepdims=True))
            p = jnp.exp2(s - jnp.concatenate([m_new] * (KB_TILE // LANES), axis=1))
            acc_ref[...] = jnp.exp2(m_old - m_new) * acc_ref[...] + _dot(p.astype(BF16), vc)
            m_ref[...] = m_new

        def kv_pair(i, carry):
            j = 2 * i
            scores(j + 1, s1_ref)
            accumulate(j, s0_ref)
            scores(j + 2, s0_ref)
            accumulate(j + 1, s1_ref)
            return carry

        n_chunks = seq // KB_TILE
        scores(0, s0_ref)
        lax.fori_loop(0, n_chunks // 2 - 1, kv_pair, 0)
        scores(n_chunks - 1, s1_ref)
        accumulate(n_chunks - 2, s0_ref)
        accumulate(n_chunks - 1, s1_ref)
        acc = acc_ref[...]
        res = acc / pltpu.roll(acc, HEAD_DIM, 1)
        for g in range(group):
            heads_out[n * group + g] = res[g * tq:(g + 1) * tq, :]
    for c in range(WIDTH_B // LANES):
        even, odd = heads_out[2 * c], heads_out[2 * c + 1]
        n = (2 * c) // group
        if n == 0:
            odd = pltpu.roll(odd, HEAD_DIM, 1)
        else:
            even = pltpu.roll(even, HEAD_DIM, 1)
        o_ref[0, :, c * LANES:(c + 1) * LANES] = jnp.where(lane_half == 0, even, odd)


def _gqa_attention(q, k, v, batch, seq):
    tq = QB_TILE
    rows = (N_HEADS_B // N_KV_B) * tq
    o = pl.pallas_call(
        functools.partial(_gqa_kernel, seq=seq),
        grid=(batch, seq // tq),
        in_specs=[pl.BlockSpec((1, tq, WIDTH_B), lambda b, i: (b, i, 0)),
                  pl.BlockSpec((1, seq, KV_WIDTH_B), lambda b, i: (b, 0, 0)),
                  pl.BlockSpec((1, seq, 2 * KV_WIDTH_B), lambda b, i: (b, 0, 0))],
        out_specs=pl.BlockSpec((1, tq, WIDTH_B), lambda b, i: (b, i, 0)),
        out_shape=jax.ShapeDtypeStruct((batch, seq, WIDTH_B), F32),
        scratch_shapes=[pltpu.VMEM((rows, LANES), BF16), pltpu.VMEM((rows, LANES), F32),
                        pltpu.VMEM((rows, LANES), F32), pltpu.VMEM((rows, KB_TILE), F32),
                        pltpu.VMEM((rows, KB_TILE), F32)],
        name="gqa_attn",
        compiler_params=_cparams(("parallel", "parallel")),
    )(q.reshape(batch, seq, WIDTH_B), k.reshape(batch, seq, KV_WIDTH_B), v.reshape(batch, seq, 2 * KV_WIDTH_B))
    return o.reshape(batch * seq, WIDTH_B)


def _post_kernel(o1_ref, o2_ref, o3_ref, l1_ref, l2_ref, l3_ref, ob_ref, x_ref, ga_ref, gb_ref, wo_ref,
                 gf_ref, wr_ref, x1_ref, h2_ref, afft_ref):
    l1, l2, l3 = l1_ref[...], l2_ref[...], l3_ref[...]
    mx = jnp.maximum(jnp.maximum(l1, l2), l3)
    e1, e2, e3 = jnp.exp(l1 - mx), jnp.exp(l2 - mx), jnp.exp(l3 - mx)
    oa = (e1 * o1_ref[...] + e2 * o2_ref[...] + e3 * o3_ref[...]) / (e1 + e2 + e3)
    na = _rms(oa, ga_ref[...]).astype(BF16)
    nb = _rms(ob_ref[...], gb_ref[...]).astype(BF16)
    mixed = _dot(na, wo_ref[0:WIDTH_A, :]) + _dot(nb, wo_ref[WIDTH_A:WIDTH_A + WIDTH_B, :])
    x1 = x_ref[...] + mixed
    x1_ref[...] = x1
    h2 = _rms(x1, gf_ref[...])
    logits = _dot(h2.astype(BF16), wr_ref[...])
    lane = lax.broadcasted_iota(jnp.int32, logits.shape, 1)
    logits = jnp.where(lane < N_EXPERTS, logits, -jnp.inf)
    ex = jnp.exp(logits - jnp.max(logits, axis=-1, keepdims=True))
    aff = ex / jnp.sum(ex, axis=-1, keepdims=True)
    h2_ref[:, 0:D_MODEL] = h2
    h2_ref[:, D_MODEL:D_MODEL + AFF_PAD] = aff
    afft_ref[...] = aff.T[0:N_EXPERTS, :]


def _post_attention(oas, lses, ob, x2d, norm_out_a, norm_out_b, w_out_bf, norm_ffn, w_router_bf):
    n = x2d.shape[0]
    tm = ROW_TILE
    row = lambda i: (i, 0)
    fixed = lambda i: (0, 0)
    half = pl.BlockSpec((tm, WIDTH_A), row)
    wr_pad = jnp.zeros((D_MODEL, AFF_PAD), BF16).at[:, :N_EXPERTS].set(w_router_bf)
    return pl.pallas_call(
        _post_kernel,
        grid=(n // tm,),
        in_specs=[half] * 7 + [pl.BlockSpec((tm, D_MODEL), row), pl.BlockSpec((1, WIDTH_A), fixed),
                               pl.BlockSpec((1, WIDTH_B), fixed), pl.BlockSpec((WIDTH_A + WIDTH_B, D_MODEL), fixed),
                               pl.BlockSpec((1, D_MODEL), fixed), pl.BlockSpec((D_MODEL, AFF_PAD), fixed)],
        out_specs=[pl.BlockSpec((tm, D_MODEL), row), pl.BlockSpec((tm, D_MODEL + AFF_PAD), row),
                   pl.BlockSpec((N_EXPERTS, tm), lambda i: (0, i))],
        out_shape=(jax.ShapeDtypeStruct((n, D_MODEL), F32), jax.ShapeDtypeStruct((n, D_MODEL + AFF_PAD), F32),
                   jax.ShapeDtypeStruct((N_EXPERTS, n), F32)),
        name="post_attn",
        compiler_params=_cparams(("parallel",)),
    )(*oas, *lses, ob, x2d, norm_out_a.reshape(1, WIDTH_A), norm_out_b.reshape(1, WIDTH_B), w_out_bf,
      norm_ffn.reshape(1, D_MODEL), wr_pad)


def _select_kernel(aff_ref, idx_ref, *, cap):
    ne, nch, _ = aff_ref.shape
    bits = pltpu.bitcast(aff_ref[...], jnp.int32)

    def count(mask):
        return jnp.sum(jnp.sum(mask.astype(F32), axis=1, keepdims=True), axis=2, keepdims=True)

    def search(i, thr):
        cand = thr | jnp.left_shift(jnp.int32(1), 30 - i)
        return jnp.where(count(bits >= cand) >= cap, cand, thr)

    thr = lax.fori_loop(0, 31, search, jnp.zeros((ne, 1, 1), jnp.int32))

    ci = lax.broadcasted_iota(jnp.int32, (LANES, LANES), 0)
    cj = lax.broadcasted_iota(jnp.int32, (LANES, LANES), 1)
    upper = (ci <= cj).astype(BF16)
    ones = jnp.ones((LANES, LANES), BF16)
    ri = lax.broadcasted_iota(jnp.int32, (nch, nch), 0)
    rj = lax.broadcasted_iota(jnp.int32, (nch, nch), 1)
    strict_lower = (rj < ri).astype(BF16)

    def prefix(mask_f):
        mb = mask_f.astype(BF16).reshape(ne * nch, LANES)
        local = _dot(mb, upper).reshape(ne, nch, LANES)
        tot = _dot(mb, ones).reshape(ne, nch, LANES)
        offs = jnp.stack([_dot(strict_lower, tot[e].astype(BF16)) for e in range(ne)])
        return local, tot, offs

    gt = bits > thr
    eq = bits == thr
    need = cap - count(gt)
    eq_local, _, eq_offs = prefix(eq.astype(F32))
    sel = jnp.logical_or(gt, jnp.logical_and(eq, (eq_local + eq_offs) <= need)).astype(F32)
    loc, tot, offs = prefix(sel)

    slot = lax.broadcasted_iota(jnp.int32, (1, cap), 1).astype(F32)
    chunk_id = lax.broadcasted_iota(jnp.int32, (nch, 1), 0).astype(F32)
    for e in range(ne):
        before = offs[e][:, 0:1]
        through = before + tot[e][:, 0:1]
        chunk_of = jnp.sum((through <= slot).astype(F32), axis=0, keepdims=True)
        onehot = chunk_id == chunk_of
        base = jnp.sum(jnp.where(onehot, before, 0.0), axis=0, keepdims=True)
        rows = _dot(loc[e].T.astype(BF16), onehot.astype(BF16))
        lane_of = jnp.sum((rows <= slot - base).astype(F32), axis=0, keepdims=True)
        idx_ref[e:e + 1, :] = (chunk_of * LANES + lane_of).astype(jnp.int32)


def _select(aff_t, cap):
    ne, n = aff_t.shape
    return pl.pallas_call(
        functools.partial(_select_kernel, cap=cap),
        out_shape=jax.ShapeDtypeStruct((ne, cap), jnp.int32),
        name="ec_select",
        compiler_params=pltpu.CompilerParams(vmem_limit_bytes=VMEM_LIMIT),
    )(aff_t.reshape(ne, n // LANES, LANES))


def _expert_kernel(idx_ref, idxp_ref, idxn_ref, h2_hbm, wg_ref, wu_ref, wd_ref, acc_in_hbm, out_hbm,
                   xbuf, xe, gate, acc, gbuf, sbuf, sem_x, sem_g, sem_s, *, cap, chunk, n_tok):
    del acc_in_hbm
    e = pl.program_id(0)
    f = pl.program_id(1)
    n_exp = pl.num_programs(0)
    last = pl.num_programs(1) - 1
    capp = xbuf.shape[0]
    par = e % 2

    def x_copy(tok, j):
        return pltpu.make_async_copy(h2_hbm.at[pl.ds(jnp.minimum(tok, n_tok - 1), 1), :], xbuf.at[pl.ds(j, 1), :],
                                     sem_x)

    def g_copy(tok, j):
        return pltpu.make_async_copy(out_hbm.at[par, pl.ds(tok, 1), :], gbuf.at[pl.ds(j, 1), :], sem_g)

    def s_copy(tok, j, parity):
        return pltpu.make_async_copy(sbuf.at[pl.ds(j, 1), :], out_hbm.at[parity, pl.ds(tok, 1), :], sem_s)

    def all_rows(issue):
        def body(jo, carry):
            for u in range(8):
                issue(jo * 8 + u)
            return carry
        lax.fori_loop(0, capp // 8, body, 0)

    @pl.when(jnp.logical_and(e == 0, f == 0))
    def _first_expert():
        all_rows(lambda j: x_copy(idx_ref[0, 0, j], j).start())
        sbuf[...] = jnp.zeros(sbuf.shape, F32)

    @pl.when(f == 0)
    def _start_expert():
        pltpu.make_async_copy(h2_hbm.at[pl.ds(0, capp), :], xbuf, sem_x).wait()
        xe[...] = xbuf[0:cap, 0:D_MODEL].astype(BF16)
        aff = xbuf[0:cap, D_MODEL:D_MODEL + AFF_PAD]
        lane = lax.broadcasted_iota(jnp.int32, aff.shape, 1)
        gate[...] = jnp.sum(jnp.where(lane == e, aff, 0.0), axis=-1, keepdims=True)
        acc[...] = jnp.zeros(acc.shape, F32)

    base = f * chunk
    for u in range(chunk):
        j = base + u
        x_copy(idxn_ref[0, 0, j], j).start()
        g_copy(idx_ref[0, 0, j], j).start()
        s_copy(idxp_ref[0, 0, j], j, 1 - par).start()

    wg = wg_ref[0].astype(BF16)
    wu = wu_ref[0].astype(BF16)
    wd = wd_ref[0].astype(BF16)
    rc = min(EXPERT_ROW_CHUNK, cap)
    for r in range(cap // rc):
        rows = slice(r * rc, (r + 1) * rc)
        x = xe[rows, :]
        g = _dot(x, wg)
        hid = (g * (1.0 / (1.0 + jnp.exp(-g))) * _dot(x, wu)).astype(BF16)
        acc[rows, :] += _dot(hid, wd)

    @pl.when(f == last)
    def _finish_expert():
        pltpu.make_async_copy(out_hbm.at[0, pl.ds(0, capp), :], gbuf, sem_g).wait()
        pltpu.make_async_copy(sbuf, out_hbm.at[0, pl.ds(0, capp), :], sem_s).wait()
        sbuf[0:cap, :] = gbuf[0:cap, :] + acc[...] * gate[...]
        sbuf[cap:capp, :] = gbuf[cap:capp, :]

    @pl.when(jnp.logical_and(f == last, e == n_exp - 1))
    def _last_expert():
        all_rows(lambda j: s_copy(idx_ref[0, 0, j], j, par).start())
        pltpu.make_async_copy(sbuf, out_hbm.at[0, pl.ds(0, capp), :], sem_s).wait()
        pltpu.make_async_copy(h2_hbm.at[pl.ds(0, capp), :], xbuf, sem_x).wait()


def _experts(idx, h2_ext, x1, w_gate, w_up, w_down, cap):
    n = x1.shape[0]
    nf = EXPERT_FF // FF_TILE
    chunk = -(-cap // (nf * 8)) * 8
    capp = chunk * nf
    pad = capp - cap
    idx_pad = jnp.concatenate([idx, jnp.broadcast_to(n + jnp.arange(pad, dtype=jnp.int32), (N_EXPERTS, pad))], axis=1)
    idx_pad = idx_pad.reshape(N_EXPERTS, 1, capp)
    acc2 = jnp.zeros((2, n + pad, D_MODEL), F32).at[0, :n].set(x1)
    idx_spec = lambda fn: pl.BlockSpec((1, 1, capp), fn, memory_space=pltpu.SMEM)
    return pl.pallas_call(
        functools.partial(_expert_kernel, cap=cap, chunk=chunk, n_tok=n),
        grid=(N_EXPERTS, nf),
        in_specs=[idx_spec(lambda e, f: (e, 0, 0)),
                  idx_spec(lambda e, f: (jnp.maximum(e - 1, 0), 0, 0)),
                  idx_spec(lambda e, f: (jnp.minimum(e + 1, N_EXPERTS - 1), 0, 0)),
                  pl.BlockSpec(memory_space=pl.ANY),
                  pl.BlockSpec((1, D_MODEL, FF_TILE), lambda e, f: (e, 0, f)),
                  pl.BlockSpec((1, D_MODEL, FF_TILE), lambda e, f: (e, 0, f)),
                  pl.BlockSpec((1, FF_TILE, D_MODEL), lambda e, f: (e, f, 0)),
                  pl.BlockSpec(memory_space=pl.ANY)],
        out_specs=pl.BlockSpec(memory_space=pl.ANY),
        out_shape=jax.ShapeDtypeStruct((2, n + pad, D_MODEL), F32),
        scratch_shapes=[pltpu.VMEM((capp, D_MODEL + AFF_PAD), F32), pltpu.VMEM((cap, D_MODEL), BF16),
                        pltpu.VMEM((cap, 1), F32), pltpu.VMEM((cap, D_MODEL), F32),
                        pltpu.VMEM((capp, D_MODEL), F32), pltpu.VMEM((capp, D_MODEL), F32),
                        pltpu.SemaphoreType.DMA(()), pltpu.SemaphoreType.DMA(()), pltpu.SemaphoreType.DMA(())],
        input_output_aliases={7: 0},
        name="experts",
        compiler_params=pltpu.CompilerParams(dimension_semantics=("arbitrary", "arbitrary"),
                                             vmem_limit_bytes=EXPERT_VMEM_LIMIT),
    )(idx_pad, idx_pad, idx_pad, h2_ext, w_gate, w_up, w_down, acc2)


def _final_kernel(a_ref, b_ref, g_ref, o_ref):
    o_ref[...] = _rms(a_ref[0] + b_ref[0], g_ref[...])


def _final_norm(acc2, n, gain):
    tm = ROW_TILE
    return pl.pallas_call(
        _final_kernel,
        grid=(n // tm,),
        in_specs=[pl.BlockSpec((1, tm, D_MODEL), lambda i: (0, i, 0)),
                  pl.BlockSpec((1, tm, D_MODEL), lambda i: (1, i, 0)),
                  pl.BlockSpec((1, D_MODEL), lambda i: (0, 0))],
        out_specs=pl.BlockSpec((tm, D_MODEL), lambda i: (i, 0)),
        out_shape=jax.ShapeDtypeStruct((n, D_MODEL), F32),
        name="final_norm",
        compiler_params=_cparams(("parallel",)),
    )(acc2, acc2, gain.reshape(1, D_MODEL))


def _trunk(x, norm_mix, w_in, q_norm_b, k_norm_b, norm_out_a, norm_out_b, w_out, norm_ffn, w_router,
           w_gate, w_up, w_down, norm_final):
    batch, seq, _ = x.shape
    n = batch * seq
    cap = EC_CAPACITY_FACTOR * n // N_EXPERTS
    x2d = x.reshape(n, D_MODEL)
    tables = _rope_tables(seq)
    assert norm_mix.shape[0] == 1, "single trunk layer"
    l = 0
    qa, ka, va, qb, kb, vb = _project(x2d, seq, norm_mix[l], w_in[l].astype(BF16), q_norm_b[l], k_norm_b[l], tables)
    pats = [_banded_attention(qa, ka, va, batch, seq, d) for d in DILATIONS]
    ob = _gqa_attention(qb, kb, vb, batch, seq)
    x1, h2_ext, aff_t = _post_attention([p[0] for p in pats], [p[1] for p in pats], ob, x2d, norm_out_a[l],
                                        norm_out_b[l], w_out[l].astype(BF16), norm_ffn[l], w_router[l].astype(BF16))
    idx = _select(aff_t, cap)
    acc2 = _experts(idx, h2_ext, x1, w_gate[l], w_up[l], w_down[l], cap)
    return _final_norm(acc2, n, norm_final).reshape(batch, seq, D_MODEL)


def kernel(x_prompt, x_sample, norm_mix, w_in, q_norm_b, k_norm_b, norm_out_a, norm_out_b, w_out, norm_ffn,
           w_router, w_gate, w_up, w_down, norm_final):
    params = (norm_mix, w_in, q_norm_b, k_norm_b, norm_out_a, norm_out_b, w_out, norm_ffn, w_router, w_gate,
              w_up, w_down, norm_final)
    return (_trunk(x_prompt, *params), _trunk(x_sample, *params))
```

```python
import functools

import jax
import jax.numpy as jnp
from jax import lax
from jax.experimental import pallas as pl
from jax.experimental.pallas import tpu as pltpu

F32 = jnp.float32
BF16 = jnp.bfloat16

D_MODEL = 1024
HEAD_DIM = 64
N_HEADS_A = 8
N_HEADS_B = 8
N_KV_B = 2
WIDTH_A = N_HEADS_A * HEAD_DIM
WIDTH_B = N_HEADS_B * HEAD_DIM
KV_WIDTH_B = N_KV_B * HEAD_DIM
IN_WIDTH = 3 * WIDTH_A + WIDTH_B + 2 * KV_WIDTH_B
DILATIONS = (1, 4, 16)
BAND_HALF = 64
ROPE_THETA_A = 500000.0
ROT_DIM_A = HEAD_DIM // 4
ROPE_THETA_B = 10000.0
GRID_W = 64
N_EXPERTS = 16
EC_CAPACITY_FACTOR = 2
EXPERT_FF = 2816
EPS = 1e-6
NEG_INF = -1e30
SCALE = HEAD_DIM ** -0.5
LOG2E = 1.4426950408889634

LANES = 128
ROW_TILE = 256
Q_TILE = 128
QB_TILE = 128
KB_TILE = 1024
FF_TILE = 256
AFF_PAD = LANES
EXPERT_ROW_CHUNK = 512
VMEM_LIMIT = 56 * 1024 * 1024
EXPERT_VMEM_LIMIT = 60 * 1024 * 1024


def _cparams(sem):
    return pltpu.CompilerParams(dimension_semantics=sem, vmem_limit_bytes=VMEM_LIMIT)


def _rms(x, g):
    return x * lax.rsqrt(jnp.mean(x * x, axis=-1, keepdims=True) + EPS) * g


def _dot(a, b):
    return jnp.dot(a, b, preferred_element_type=F32)


def _dot_nt(a, b):
    return lax.dot_general(a, b, (((1,), (1,)), ((), ())), preferred_element_type=F32)


def _rope(v, c, s_next, s_prev, shift):
    return v * c + pltpu.roll(v, LANES - shift, 1) * s_next + pltpu.roll(v, shift, 1) * s_prev


def _head_norm(v, gain, ones_bd):
    sq = v * v
    hi = sq.astype(BF16)
    lo = (sq - hi.astype(F32)).astype(BF16)
    ss = _dot(hi, ones_bd) + _dot(lo, ones_bd)
    return v * lax.rsqrt(ss * (1.0 / HEAD_DIM) + EPS) * gain


def _store_residue_views(stage_ref, view_refs):
    cols, rows, _ = stage_ref.shape
    for dil, ref in zip(DILATIONS, view_refs):
        for r in range(dil):
            for c in range(cols):
                part = stage_ref[c] if dil == 1 else stage_ref[c, pl.ds(r, rows // dil, stride=dil), :]
                lane0 = (r * cols + c) * LANES
                ref[0, :, lane0:lane0 + LANES] = part.astype(BF16)


def _proj_kernel(x_ref, g_ref, w_ref, qn_ref, kn_ref, ca_ref, sa1_ref, sa2_ref, cb_ref, sb1_ref, sb2_ref,
                 ones_ref, *refs):
    nd = len(DILATIONS)
    qa_refs, ka_refs, va_refs = refs[0:nd], refs[nd:2 * nd], refs[2 * nd:3 * nd]
    qb_ref, kb_ref, vb_ref, stage_ref = refs[3 * nd:]
    h = _rms(x_ref[...], g_ref[...]).astype(BF16)
    ca, sa1, sa2 = ca_ref[...], sa1_ref[...], sa2_ref[...]
    cb, sb1, sb2 = cb_ref[...], sb1_ref[...], sb2_ref[...]
    ones_bd = ones_ref[...]
    o_ka, o_va, o_qb, o_kb, o_vb = WIDTH_A, 2 * WIDTH_A, 3 * WIDTH_A, 3 * WIDTH_A + WIDTH_B, 3 * WIDTH_A + WIDTH_B + KV_WIDTH_B

    p = _dot(h, w_ref[:, 0:o_ka])
    for c in range(WIDTH_A // LANES):
        stage_ref[c] = _rope(p[:, c * LANES:(c + 1) * LANES], ca, sa1, sa2, ROT_DIM_A // 2) * SCALE
    _store_residue_views(stage_ref, qa_refs)
    p = _dot(h, w_ref[:, o_ka:o_va])
    for c in range(WIDTH_A // LANES):
        stage_ref[c] = _rope(p[:, c * LANES:(c + 1) * LANES], ca, sa1, sa2, ROT_DIM_A // 2)
    _store_residue_views(stage_ref, ka_refs)
    p = _dot(h, w_ref[:, o_va:o_qb])
    for c in range(WIDTH_A // LANES):
        stage_ref[c] = p[:, c * LANES:(c + 1) * LANES]
    _store_residue_views(stage_ref, va_refs)

    p = _dot(h, w_ref[:, o_qb:o_kb])
    qn = qn_ref[...]
    for c in range(WIDTH_B // LANES):
        sl = slice(c * LANES, (c + 1) * LANES)
        v = _head_norm(p[:, sl], qn, ones_bd)
        qb_ref[:, sl] = (_rope(v, cb, sb1, sb2, HEAD_DIM // 4) * (SCALE * LOG2E)).astype(BF16)
    p = _dot(h, w_ref[:, o_kb:o_vb])
    v = _head_norm(p, kn_ref[...], ones_bd)
    kb_ref[...] = _rope(v, cb, sb1, sb2, HEAD_DIM // 4).astype(BF16)
    vb = _dot(h, w_ref[:, o_vb:IN_WIDTH])
    lo_half = lax.broadcasted_iota(jnp.int32, vb.shape, 1) < HEAD_DIM
    vb_ref[:, 0:LANES] = jnp.where(lo_half, vb, 1.0).astype(BF16)
    vb_ref[:, LANES:2 * LANES] = jnp.where(lo_half, 1.0, vb).astype(BF16)


def _rope_tables(seq):
    lane = jnp.arange(LANES) % HEAD_DIM
    pos = jnp.arange(seq)

    def cos_sin(p, dim, theta):
        inv = theta ** (-jnp.arange(0, dim, 2, dtype=F32) / dim)
        ang = p.astype(F32)[:, None] * inv[None, :]
        return jnp.cos(ang), jnp.sin(ang)

    ha = ROT_DIM_A // 2
    cos_a, sin_a = cos_sin(pos, ROT_DIM_A, ROPE_THETA_A)
    fa = lane % ha
    in_lo, in_hi = lane < ha, (lane >= ha) & (lane < ROT_DIM_A)
    ca = jnp.where((in_lo | in_hi)[None, :], cos_a[:, fa], 1.0)
    sa1 = jnp.where(in_lo[None, :], -sin_a[:, fa], 0.0)
    sa2 = jnp.where(in_hi[None, :], sin_a[:, fa], 0.0)
    hb = HEAD_DIM // 4
    cos_r, sin_r = cos_sin(pos // GRID_W, HEAD_DIM // 2, ROPE_THETA_B)
    cos_c, sin_c = cos_sin(pos % GRID_W, HEAD_DIM // 2, ROPE_THETA_B)
    fb = lane % hb
    is_row = (lane < HEAD_DIM // 2)[None, :]
    cosb = jnp.where(is_row, cos_r[:, fb], cos_c[:, fb])
    sinb = jnp.where(is_row, sin_r[:, fb], sin_c[:, fb])
    first = ((lane // hb) % 2 == 0)[None, :]
    sb1 = jnp.where(first, -sinb, 0.0)
    sb2 = jnp.where(first, 0.0, sinb)
    return ca, sa1, sa2, cosb, sb1, sb2


def _project(x2d, seq, norm_mix, w_in_bf, q_norm, k_norm, tables):
    n = x2d.shape[0]
    tm = ROW_TILE
    tiles_per_seq = seq // tm
    row = lambda i: (i, 0)
    fixed = lambda i: (0, 0)
    tab = lambda i: (i % tiles_per_seq, 0)
    ones_bd = (jnp.arange(LANES)[:, None] // HEAD_DIM == jnp.arange(LANES)[None, :] // HEAD_DIM).astype(BF16)
    gain2 = lambda g: jnp.tile(g.reshape(1, HEAD_DIM), (1, LANES // HEAD_DIM))
    tab_spec = pl.BlockSpec((tm, LANES), tab)
    batch = n // seq
    view_shapes = [jax.ShapeDtypeStruct((batch, seq // d, d * WIDTH_A), BF16) for d in DILATIONS] * 3
    view_specs = [pl.BlockSpec((1, tm // d, d * WIDTH_A), lambda i: (i // tiles_per_seq, i % tiles_per_seq, 0))
                  for d in DILATIONS] * 3
    out_shapes = view_shapes + [
        jax.ShapeDtypeStruct((n, WIDTH_B), BF16), jax.ShapeDtypeStruct((n, KV_WIDTH_B), BF16),
        jax.ShapeDtypeStruct((n, 2 * KV_WIDTH_B), BF16)]
    outs = pl.pallas_call(
        _proj_kernel,
        grid=(n // tm,),
        in_specs=[pl.BlockSpec((tm, D_MODEL), row), pl.BlockSpec((1, D_MODEL), fixed),
                  pl.BlockSpec((D_MODEL, IN_WIDTH), fixed), pl.BlockSpec((1, LANES), fixed),
                  pl.BlockSpec((1, LANES), fixed)] + [tab_spec] * 6 + [pl.BlockSpec((LANES, LANES), fixed)],
        out_specs=view_specs + [pl.BlockSpec((tm, WIDTH_B), row), pl.BlockSpec((tm, KV_WIDTH_B), row),
                                pl.BlockSpec((tm, 2 * KV_WIDTH_B), row)],
        out_shape=out_shapes,
        scratch_shapes=[pltpu.VMEM((WIDTH_A // LANES, tm, LANES), F32)],
        name="in_proj",
        compiler_params=_cparams(("parallel",)),
    )(x2d, norm_mix.reshape(1, D_MODEL), w_in_bf, gain2(q_norm), gain2(k_norm), *tables, ones_bd)
    nd = len(DILATIONS)
    return outs[0:nd], outs[nd:2 * nd], outs[2 * nd:3 * nd], outs[3 * nd], outs[3 * nd + 1], outs[3 * nd + 2]


def _band_kernel(q_ref, kp_ref, kc_ref, kn_ref, vp_ref, vc_ref, vn_ref, o_ref, lse_ref, *, sub_len):
    i = pl.program_id(2)
    t = Q_TILE
    q = q_ref[0]
    kw = jnp.concatenate([kp_ref[0, t - BAND_HALF:t], kc_ref[0], kn_ref[0, 0:BAND_HALF]], axis=0)
    vw = jnp.concatenate([vp_ref[0, t - BAND_HALF:t], vc_ref[0], vn_ref[0, 0:BAND_HALF]], axis=0)
    nk = t + 2 * BAND_HALF
    row = lax.broadcasted_iota(jnp.int32, (t, nk), 0)
    col = lax.broadcasted_iota(jnp.int32, (t, nk), 1)
    kj = i * t - BAND_HALF + col
    rel = col - BAND_HALF - row
    valid = (kj >= 0) & (kj < sub_len) & (rel >= -BAND_HALF) & (rel <= BAND_HALF)
    lo_half = lax.broadcasted_iota(jnp.int32, (t, LANES), 1) < HEAD_DIM
    zero = jnp.zeros((), BF16)
    for c in range(WIDTH_A // LANES):
        sl = slice(c * LANES, (c + 1) * LANES)
        qc, kc, vc = q[:, sl], kw[:, sl], vw[:, sl]
        outs, lses = [], []
        for half_mask in (lo_half, jnp.logical_not(lo_half)):
            s = _dot_nt(jnp.where(half_mask, qc, zero), kc)
            s = jnp.where(valid, s, NEG_INF)
            m = jnp.max(s, axis=-1, keepdims=True)
            p = jnp.exp(s - m)
            l = jnp.sum(p, axis=-1, keepdims=True)
            outs.append(_dot(p.astype(BF16), vc) / l)
            lses.append(jnp.broadcast_to(m + jnp.log(l), (t, LANES)))
        o_ref[0, :, sl] = jnp.where(lo_half, outs[0], outs[1])
        lse_ref[0, :, sl] = jnp.where(lo_half, lses[0], lses[1])


def _banded_attention(q, k, v, batch, seq, dil):
    sub_len = seq // dil
    nb = sub_len // Q_TILE
    blk = (1, Q_TILE, WIDTH_A)
    cur = pl.BlockSpec(blk, lambda b, r, i: (b, i, r))
    prev = pl.BlockSpec(blk, lambda b, r, i: (b, jnp.maximum(i - 1, 0), r))
    nxt = pl.BlockSpec(blk, lambda b, r, i: (b, jnp.minimum(i + 1, nb - 1), r))
    shape = jax.ShapeDtypeStruct((batch, sub_len, dil * WIDTH_A), F32)
    o, lse = pl.pallas_call(
        functools.partial(_band_kernel, sub_len=sub_len),
        grid=(batch, dil, nb),
        in_specs=[cur, prev, cur, nxt, prev, cur, nxt],
        out_specs=[cur, cur],
        out_shape=(shape, shape),
        name=f"band_attn_d{dil}",
        compiler_params=_cparams(("parallel", "parallel", "parallel")),
    )(q, k, k, k, v, v, v)
    return o, lse


def _gqa_kernel(q_ref, k_ref, v_ref, o_ref, qs_ref, m_ref, acc_ref, s0_ref, s1_ref, *, seq):
    tq = QB_TILE
    group = N_HEADS_B // N_KV_B
    lane_half = lax.broadcasted_iota(jnp.int32, (tq, LANES), 1) // HEAD_DIM
    heads_out = [None] * N_HEADS_B
    for n in range(N_KV_B):
        for g in range(group):
            hd = n * group + g
            qc = q_ref[0, :, (hd // 2) * LANES:(hd // 2 + 1) * LANES].astype(F32)
            if hd % 2 != n:
                qc = pltpu.roll(qc, HEAD_DIM, 1)
            qs_ref[g * tq:(g + 1) * tq, :] = jnp.where(lane_half == n, qc, 0.0).astype(BF16)
        m_ref[...] = jnp.full(m_ref.shape, -jnp.inf, F32)
        acc_ref[...] = jnp.zeros(acc_ref.shape, F32)

        def scores(j, dst_ref):
            start = pl.multiple_of(j * KB_TILE, KB_TILE)
            dst_ref[...] = _dot_nt(qs_ref[...], k_ref[0, pl.ds(start, KB_TILE), :])

        def accumulate(j, src_ref):
            start = pl.multiple_of(j * KB_TILE, KB_TILE)
            vc = v_ref[0, pl.ds(start, KB_TILE), n * LANES:(n + 1) * LANES]
            s = src_ref[...]
            m_old = m_ref[...]
            m_new = jnp.maximum(m_old, jnp.max(s, axis=-1, keepdims=True))
            p = jnp.exp2(s - jnp.concatenate([m_new] * (KB_TILE // LANES), axis=1))
            acc_ref[...] = jnp.exp2(m_old - m_new) * acc_ref[...] + _dot(p.astype(BF16), vc)
            m_ref[...] = m_new

        def kv_pair(i, carry):
            j = 2 * i
            scores(j + 1, s1_ref)
            accumulate(j, s0_ref)
            scores(j + 2, s0_ref)
            accumulate(j + 1, s1_ref)
            return carry

        n_chunks = seq // KB_TILE
        scores(0, s0_ref)
        lax.fori_loop(0, n_chunks // 2 - 1, kv_pair, 0)
        scores(n_chunks - 1, s1_ref)
        accumulate(n_chunks - 2, s0_ref)
        accumulate(n_chunks - 1, s1_ref)
        acc = acc_ref[...]
        res = acc / pltpu.roll(acc, HEAD_DIM, 1)
        for g in range(group):
            heads_out[n * group + g] = res[g * tq:(g + 1) * tq, :]
    for c in range(WIDTH_B // LANES):
        even, odd = heads_out[2 * c], heads_out[2 * c + 1]
        n = (2 * c) // group
        if n == 0:
            odd = pltpu.roll(odd, HEAD_DIM, 1)
        else:
            even = pltpu.roll(even, HEAD_DIM, 1)
        o_ref[0, :, c * LANES:(c + 1) * LANES] = jnp.where(lane_half == 0, even, odd)


def _gqa_attention(q, k, v, batch, seq):
    tq = QB_TILE
    rows = (N_HEADS_B // N_KV_B) * tq
    o = pl.pallas_call(
        functools.partial(_gqa_kernel, seq=seq),
        grid=(batch, seq // tq),
        in_specs=[pl.BlockSpec((1, tq, WIDTH_B), lambda b, i: (b, i, 0)),
                  pl.BlockSpec((1, seq, KV_WIDTH_B), lambda b, i: (b, 0, 0)),
                  pl.BlockSpec((1, seq, 2 * KV_WIDTH_B), lambda b, i: (b, 0, 0))],
        out_specs=pl.BlockSpec((1, tq, WIDTH_B), lambda b, i: (b, i, 0)),
        out_shape=jax.ShapeDtypeStruct((batch, seq, WIDTH_B), F32),
        scratch_shapes=[pltpu.VMEM((rows, LANES), BF16), pltpu.VMEM((rows, LANES), F32),
                        pltpu.VMEM((rows, LANES), F32), pltpu.VMEM((rows, KB_TILE), F32),
                        pltpu.VMEM((rows, KB_TILE), F32)],
        name="gqa_attn",
        compiler_params=_cparams(("parallel", "parallel")),
    )(q.reshape(batch, seq, WIDTH_B), k.reshape(batch, seq, KV_WIDTH_B), v.reshape(batch, seq, 2 * KV_WIDTH_B))
    return o.reshape(batch * seq, WIDTH_B)


def _load_residue_view(view_ref, dil, stage_ref):
    if dil == 1:
        return view_ref[0]
    cols, rows, _ = stage_ref.shape
    for r in range(dil):
        for c in range(cols):
            lane0 = (r * cols + c) * LANES
            stage_ref[c, pl.ds(r, rows // dil, stride=dil), :] = view_ref[0, :, lane0:lane0 + LANES]
    return jnp.concatenate([stage_ref[c] for c in range(cols)], axis=1)


def _post_kernel(*refs):
    nd = len(DILATIONS)
    o_refs, l_refs = refs[0:nd], refs[nd:2 * nd]
    ob_ref, x_ref, ga_ref, gb_ref, wo_ref, gf_ref, wr_ref, x1_ref, h2_ref, afft_ref = refs[2 * nd:2 * nd + 10]
    stages = refs[2 * nd + 10:]
    lses = [_load_residue_view(ref, d, stages[2 * i]) for i, (ref, d) in enumerate(zip(l_refs, DILATIONS))]
    outs = [_load_residue_view(ref, d, stages[2 * i + 1]) for i, (ref, d) in enumerate(zip(o_refs, DILATIONS))]
    mx = functools.reduce(jnp.maximum, lses)
    ws = [jnp.exp(l - mx) for l in lses]
    add = lambda a, b: a + b
    oa = functools.reduce(add, [w * o for w, o in zip(ws, outs)]) / functools.reduce(add, ws)
    na = _rms(oa, ga_ref[...]).astype(BF16)
    nb = _rms(ob_ref[...], gb_ref[...]).astype(BF16)
    mixed = _dot(na, wo_ref[0:WIDTH_A, :]) + _dot(nb, wo_ref[WIDTH_A:WIDTH_A + WIDTH_B, :])
    x1 = x_ref[...] + mixed
    x1_ref[0] = x1
    x1_ref[1] = jnp.zeros(x1.shape, F32)
    h2 = _rms(x1, gf_ref[...])
    logits = _dot(h2.astype(BF16), wr_ref[...])
    lane = lax.broadcasted_iota(jnp.int32, logits.shape, 1)
    logits = jnp.where(lane < N_EXPERTS, logits, -jnp.inf)
    ex = jnp.exp(logits - jnp.max(logits, axis=-1, keepdims=True))
    aff = ex / jnp.sum(ex, axis=-1, keepdims=True)
    h2_ref[:, 0:D_MODEL] = h2
    h2_ref[:, D_MODEL:D_MODEL + AFF_PAD] = aff
    afft_ref[...] = aff.T[0:N_EXPERTS, :]


def _post_attention(oas, lses, ob, x2d, seq, acc_rows, norm_out_a, norm_out_b, w_out_bf, norm_ffn, w_router_bf):
    n = x2d.shape[0]
    tm = ROW_TILE
    tiles_per_seq = seq // tm
    n_tiles = n // tm
    extra = -(-(acc_rows - n) // tm)
    tile = lambda i: jnp.minimum(i, n_tiles - 1)
    row = lambda i: (tile(i), 0)
    fixed = lambda i: (0, 0)
    views = [pl.BlockSpec((1, tm // d, d * WIDTH_A), lambda i: (tile(i) // tiles_per_seq, tile(i) % tiles_per_seq, 0))
             for d in DILATIONS]
    wr_pad = jnp.zeros((D_MODEL, AFF_PAD), BF16).at[:, :N_EXPERTS].set(w_router_bf)
    return pl.pallas_call(
        _post_kernel,
        grid=(n_tiles + extra,),
        in_specs=views + views + [pl.BlockSpec((tm, WIDTH_B), row), pl.BlockSpec((tm, D_MODEL), row),
                                  pl.BlockSpec((1, WIDTH_A), fixed), pl.BlockSpec((1, WIDTH_B), fixed),
                                  pl.BlockSpec((WIDTH_A + WIDTH_B, D_MODEL), fixed),
                                  pl.BlockSpec((1, D_MODEL), fixed), pl.BlockSpec((D_MODEL, AFF_PAD), fixed)],
        out_specs=[pl.BlockSpec((2, tm, D_MODEL), lambda i: (0, i, 0)), pl.BlockSpec((tm, D_MODEL + AFF_PAD), row),
                   pl.BlockSpec((N_EXPERTS, tm), lambda i: (0, tile(i)))],
        out_shape=(jax.ShapeDtypeStruct((2, acc_rows, D_MODEL), F32), jax.ShapeDtypeStruct((n, D_MODEL + AFF_PAD), F32),
                   jax.ShapeDtypeStruct((N_EXPERTS, n), F32)),
        scratch_shapes=[pltpu.VMEM((WIDTH_A // LANES, tm, LANES), F32)] * (2 * len(DILATIONS)),
        name="post_attn",
        compiler_params=_cparams(("arbitrary",)),
    )(*oas, *lses, ob, x2d, norm_out_a.reshape(1, WIDTH_A), norm_out_b.reshape(1, WIDTH_B), w_out_bf,
      norm_ffn.reshape(1, D_MODEL), wr_pad)


def _select_kernel(aff_ref, idx_ref, *, cap):
    ne, nch, _ = aff_ref.shape
    aff = aff_ref[...]

    def count(mask):
        return jnp.sum(jnp.sum(mask.astype(F32), axis=1, keepdims=True), axis=2, keepdims=True)

    def search(i, thr_bits):
        cand = thr_bits | jnp.left_shift(jnp.int32(1), 30 - i)
        return jnp.where(count(aff >= lax.bitcast_convert_type(cand, F32)) >= cap, cand, thr_bits)

    thr = lax.bitcast_convert_type(lax.fori_loop(0, 31, search, jnp.zeros((ne, 1, 1), jnp.int32)), F32)

    ci = lax.broadcasted_iota(jnp.int32, (LANES, LANES), 0)
    cj = lax.broadcasted_iota(jnp.int32, (LANES, LANES), 1)
    upper = (ci <= cj).astype(BF16)
    ones = jnp.ones((LANES, LANES), BF16)
    ri = lax.broadcasted_iota(jnp.int32, (nch, nch), 0)
    rj = lax.broadcasted_iota(jnp.int32, (nch, nch), 1)
    strict_lower = (rj < ri).astype(BF16)

    def prefix(mask_f):
        mb = mask_f.astype(BF16).reshape(ne * nch, LANES)
        local = _dot(mb, upper).reshape(ne, nch, LANES)
        tot = _dot(mb, ones).reshape(ne, nch, LANES)
        offs = jnp.stack([_dot(strict_lower, tot[e].astype(BF16)) for e in range(ne)])
        return local, tot, offs

    gt = aff > thr
    eq = aff == thr
    need = cap - count(gt)
    eq_local, _, eq_offs = prefix(eq.astype(F32))
    sel = jnp.logical_or(gt, jnp.logical_and(eq, (eq_local + eq_offs) <= need)).astype(F32)
    loc, tot, offs = prefix(sel)

    slot = lax.broadcasted_iota(jnp.int32, (1, cap), 1).astype(F32)
    chunk_id = lax.broadcasted_iota(jnp.int32, (nch, 1), 0).astype(F32)
    for e in range(ne):
        before = offs[e][:, 0:1]
        through = before + tot[e][:, 0:1]
        chunk_of = jnp.sum((through <= slot).astype(F32), axis=0, keepdims=True)
        onehot = chunk_id == chunk_of
        base = jnp.sum(jnp.where(onehot, before, 0.0), axis=0, keepdims=True)
        rows = _dot(loc[e].T.astype(BF16), onehot.astype(BF16))
        lane_of = jnp.sum((rows <= slot - base).astype(F32), axis=0, keepdims=True)
        idx_ref[e:e + 1, :] = (chunk_of * LANES + lane_of).astype(jnp.int32)


def _select(aff_t, cap):
    ne, n = aff_t.shape
    return pl.pallas_call(
        functools.partial(_select_kernel, cap=cap),
        out_shape=jax.ShapeDtypeStruct((ne, cap), jnp.int32),
        name="ec_select",
        compiler_params=pltpu.CompilerParams(vmem_limit_bytes=VMEM_LIMIT),
    )(aff_t.reshape(ne, n // LANES, LANES))


def _expert_kernel(idx_ref, idxp_ref, idxn_ref, h2_hbm, wg_ref, wu_ref, wd_ref, acc_in_hbm, out_hbm,
                   xbuf, xe, gate, acc, gbuf, sbuf, sem_x, sem_g, sem_s, *, cap, chunk, n_tok):
    del acc_in_hbm
    e = pl.program_id(0)
    f = pl.program_id(1)
    n_exp = pl.num_programs(0)
    last = pl.num_programs(1) - 1
    capp = xbuf.shape[0]
    par = e % 2

    def x_copy(tok, j):
        return pltpu.make_async_copy(h2_hbm.at[pl.ds(jnp.minimum(tok, n_tok - 1), 1), :], xbuf.at[pl.ds(j, 1), :],
                                     sem_x)

    def g_copy(tok, j):
        return pltpu.make_async_copy(out_hbm.at[par, pl.ds(tok, 1), :], gbuf.at[pl.ds(j, 1), :], sem_g)

    def s_copy(tok, j, parity):
        return pltpu.make_async_copy(sbuf.at[pl.ds(j, 1), :], out_hbm.at[parity, pl.ds(tok, 1), :], sem_s)

    def all_rows(issue):
        def body(jo, carry):
            for u in range(8):
                issue(jo * 8 + u)
            return carry
        lax.fori_loop(0, capp // 8, body, 0)

    @pl.when(jnp.logical_and(e == 0, f == 0))
    def _first_expert():
        all_rows(lambda j: x_copy(idx_ref[0, 0, j], j).start())
        sbuf[...] = jnp.zeros(sbuf.shape, F32)

    @pl.when(f == 0)
    def _start_expert():
        pltpu.make_async_copy(h2_hbm.at[pl.ds(0, capp), :], xbuf, sem_x).wait()
        xe[...] = xbuf[0:cap, 0:D_MODEL].astype(BF16)
        aff = xbuf[0:cap, D_MODEL:D_MODEL + AFF_PAD]
        lane = lax.broadcasted_iota(jnp.int32, aff.shape, 1)
        gate[...] = jnp.sum(jnp.where(lane == e, aff, 0.0), axis=-1, keepdims=True)
        acc[...] = jnp.zeros(acc.shape, F32)

    base = f * chunk
    for u in range(chunk):
        j = base + u
        x_copy(idxn_ref[0, 0, j], j).start()
        g_copy(idx_ref[0, 0, j], j).start()
        s_copy(idxp_ref[0, 0, j], j, 1 - par).start()

    wg = wg_ref[0].astype(BF16)
    wu = wu_ref[0].astype(BF16)
    wd = wd_ref[0].astype(BF16)
    rc = min(EXPERT_ROW_CHUNK, cap)
    for r in range(cap // rc):
        rows = slice(r * rc, (r + 1) * rc)
        x = xe[rows, :]
        g = _dot(x, wg)
        hid = (g * (1.0 / (1.0 + jnp.exp(-g))) * _dot(x, wu)).astype(BF16)
        acc[rows, :] += _dot(hid, wd)

    @pl.when(f == last)
    def _finish_expert():
        pltpu.make_async_copy(out_hbm.at[0, pl.ds(0, capp), :], gbuf, sem_g).wait()
        pltpu.make_async_copy(sbuf, out_hbm.at[0, pl.ds(0, capp), :], sem_s).wait()
        sbuf[0:cap, :] = gbuf[0:cap, :] + acc[...] * gate[...]
        sbuf[cap:capp, :] = jnp.zeros((capp - cap, D_MODEL), F32)

    @pl.when(jnp.logical_and(f == last, e == n_exp - 1))
    def _last_expert():
        all_rows(lambda j: s_copy(idx_ref[0, 0, j], j, par).start())
        pltpu.make_async_copy(sbuf, out_hbm.at[0, pl.ds(0, capp), :], sem_s).wait()
        pltpu.make_async_copy(h2_hbm.at[pl.ds(0, capp), :], xbuf, sem_x).wait()


def _expert_chunking(cap):
    nf = EXPERT_FF // FF_TILE
    chunk = -(-cap // (nf * 8)) * 8
    return chunk, chunk * nf


def _experts(idx, h2_ext, acc2, n, w_gate, w_up, w_down, cap):
    nf = EXPERT_FF // FF_TILE
    chunk, capp = _expert_chunking(cap)
    pad = capp - cap
    idx_pad = jnp.concatenate([idx, jnp.broadcast_to(n + jnp.arange(pad, dtype=jnp.int32), (N_EXPERTS, pad))], axis=1)
    idx_pad = idx_pad.reshape(N_EXPERTS, 1, capp)
    idx_spec = lambda fn: pl.BlockSpec((1, 1, capp), fn, memory_space=pltpu.SMEM)
    return pl.pallas_call(
        functools.partial(_expert_kernel, cap=cap, chunk=chunk, n_tok=n),
        grid=(N_EXPERTS, nf),
        in_specs=[idx_spec(lambda e, f: (e, 0, 0)),
                  idx_spec(lambda e, f: (jnp.maximum(e - 1, 0), 0, 0)),
                  idx_spec(lambda e, f: (jnp.minimum(e + 1, N_EXPERTS - 1), 0, 0)),
                  pl.BlockSpec(memory_space=pl.ANY),
                  pl.BlockSpec((1, D_MODEL, FF_TILE), lambda e, f: (e, 0, f)),
                  pl.BlockSpec((1, D_MODEL, FF_TILE), lambda e, f: (e, 0, f)),
                  pl.BlockSpec((1, FF_TILE, D_MODEL), lambda e, f: (e, f, 0)),
                  pl.BlockSpec(memory_space=pl.ANY)],
        out_specs=pl.BlockSpec(memory_space=pl.ANY),
        out_shape=jax.ShapeDtypeStruct((2, n + pad, D_MODEL), F32),
        scratch_shapes=[pltpu.VMEM((capp, D_MODEL + AFF_PAD), F32), pltpu.VMEM((cap, D_MODEL), BF16),
                        pltpu.VMEM((cap, 1), F32), pltpu.VMEM((cap, D_MODEL), F32),
                        pltpu.VMEM((capp, D_MODEL), F32), pltpu.VMEM((capp, D_MODEL), F32),
                        pltpu.SemaphoreType.DMA(()), pltpu.SemaphoreType.DMA(()), pltpu.SemaphoreType.DMA(())],
        input_output_aliases={7: 0},
        name="experts",
        compiler_params=pltpu.CompilerParams(dimension_semantics=("arbitrary", "arbitrary"),
                                             vmem_limit_bytes=EXPERT_VMEM_LIMIT),
    )(idx_pad, idx_pad, idx_pad, h2_ext, w_gate, w_up, w_down, acc2)


def _final_kernel(a_ref, b_ref, g_ref, o_ref):
    o_ref[...] = _rms(a_ref[0] + b_ref[0], g_ref[...])


def _final_norm(acc2, n, gain):
    tm = ROW_TILE
    return pl.pallas_call(
        _final_kernel,
        grid=(n // tm,),
        in_specs=[pl.BlockSpec((1, tm, D_MODEL), lambda i: (0, i, 0)),
                  pl.BlockSpec((1, tm, D_MODEL), lambda i: (1, i, 0)),
                  pl.BlockSpec((1, D_MODEL), lambda i: (0, 0))],
        out_specs=pl.BlockSpec((tm, D_MODEL), lambda i: (i, 0)),
        out_shape=jax.ShapeDtypeStruct((n, D_MODEL), F32),
        name="final_norm",
        compiler_params=_cparams(("parallel",)),
    )(acc2, acc2, gain.reshape(1, D_MODEL))


def _trunk(x, norm_mix, w_in, q_norm_b, k_norm_b, norm_out_a, norm_out_b, w_out, norm_ffn, w_router,
           w_gate, w_up, w_down, norm_final):
    batch, seq, _ = x.shape
    n = batch * seq
    cap = EC_CAPACITY_FACTOR * n // N_EXPERTS
    x2d = x.reshape(n, D_MODEL)
    tables = _rope_tables(seq)
    assert norm_mix.shape[0] == 1, "single trunk layer"
    l = 0
    qa, ka, va, qb, kb, vb = _project(x2d, seq, norm_mix[l], w_in[l].astype(BF16), q_norm_b[l], k_norm_b[l], tables)
    pats = [_banded_attention(q, k, v, batch, seq, d) for q, k, v, d in zip(qa, ka, va, DILATIONS)]
    ob = _gqa_attention(qb, kb, vb, batch, seq)
    acc_rows = n + _expert_chunking(cap)[1] - cap
    acc2, h2_ext, aff_t = _post_attention([p[0] for p in pats], [p[1] for p in pats], ob, x2d, seq, acc_rows,
                                          norm_out_a[l], norm_out_b[l], w_out[l].astype(BF16), norm_ffn[l],
                                          w_router[l].astype(BF16))
    idx = _select(aff_t, cap)
    acc2 = _experts(idx, h2_ext, acc2, n, w_gate[l], w_up[l], w_down[l], cap)
    return _final_norm(acc2, n, norm_final).reshape(batch, seq, D_MODEL)


def kernel(x_prompt, x_sample, norm_mix, w_in, q_norm_b, k_norm_b, norm_out_a, norm_out_b, w_out, norm_ffn,
           w_router, w_gate, w_up, w_down, norm_final):
    params = (norm_mix, w_in, q_norm_b, k_norm_b, norm_out_a, norm_out_b, w_out, norm_ffn, w_router, w_gate,
              w_up, w_down, norm_final)
    return (_trunk(x_prompt, *params), _trunk(x_sample, *params))
```

```python
import functools

import jax
import jax.numpy as jnp
from jax import lax
from jax.experimental import pallas as pl
from jax.experimental.pallas import tpu as pltpu

F32 = jnp.float32
BF16 = jnp.bfloat16

D_MODEL = 1024
HEAD_DIM = 64
N_HEADS_A = 8
N_HEADS_B = 8
N_KV_B = 2
WIDTH_A = N_HEADS_A * HEAD_DIM
WIDTH_B = N_HEADS_B * HEAD_DIM
KV_WIDTH_B = N_KV_B * HEAD_DIM
IN_WIDTH = 3 * WIDTH_A + WIDTH_B + 2 * KV_WIDTH_B
DILATIONS = (1, 4, 16)
BAND_HALF = 64
ROPE_THETA_A = 500000.0
ROT_DIM_A = HEAD_DIM // 4
ROPE_THETA_B = 10000.0
GRID_W = 64
N_EXPERTS = 16
EC_CAPACITY_FACTOR = 2
EXPERT_FF = 2816
EPS = 1e-6
NEG_INF = -1e30
SCALE = HEAD_DIM ** -0.5
LOG2E = 1.4426950408889634
LN2 = 0.6931471805599453

LANES = 128
ROW_TILE = 256
Q_TILE = 128
BAND_STEP = 256
QB_TILE = 256
KB_TILE = 1024
FF_TILE = 256
AFF_PAD = LANES
EXPERT_ROW_CHUNK = 512
VMEM_LIMIT = 56 * 1024 * 1024
EXPERT_VMEM_LIMIT = 60 * 1024 * 1024


def _cparams(sem):
    return pltpu.CompilerParams(dimension_semantics=sem, vmem_limit_bytes=VMEM_LIMIT)


def _rms(x, g):
    return x * lax.rsqrt(jnp.mean(x * x, axis=-1, keepdims=True) + EPS) * g


def _dot(a, b):
    return jnp.dot(a, b, preferred_element_type=F32)


def _dot_nt(a, b):
    return lax.dot_general(a, b, (((1,), (1,)), ((), ())), preferred_element_type=F32)


def _rope(v, c, s_next, s_prev, shift):
    return v * c + pltpu.roll(v, LANES - shift, 1) * s_next + pltpu.roll(v, shift, 1) * s_prev


def _head_norm(v, gain, ones_bd):
    sq = v * v
    hi = sq.astype(BF16)
    lo = (sq - hi.astype(F32)).astype(BF16)
    ss = _dot(hi, ones_bd) + _dot(lo, ones_bd)
    return v * lax.rsqrt(ss * (1.0 / HEAD_DIM) + EPS) * gain


def _store_residue_views(stage_ref, view_refs):
    cols, rows, _ = stage_ref.shape
    for dil, ref in zip(DILATIONS, view_refs):
        for r in range(dil):
            for c in range(cols):
                part = stage_ref[c] if dil == 1 else stage_ref[c, pl.ds(r, rows // dil, stride=dil), :]
                lane0 = (r * cols + c) * LANES
                ref[0, :, lane0:lane0 + LANES] = part.astype(BF16)


def _proj_kernel(x_ref, g_ref, w_ref, qn_ref, kn_ref, ca_ref, sa1_ref, sa2_ref, cb_ref, sb1_ref, sb2_ref,
                 ones_ref, *refs):
    nd = len(DILATIONS)
    qa_refs, ka_refs, va_refs = refs[0:nd], refs[nd:2 * nd], refs[2 * nd:3 * nd]
    qb_ref, kb_ref, vb_ref, stage_ref = refs[3 * nd:]
    h = _rms(x_ref[...], g_ref[...]).astype(BF16)
    ca, sa1, sa2 = ca_ref[...], sa1_ref[...], sa2_ref[...]
    cb, sb1, sb2 = cb_ref[...], sb1_ref[...], sb2_ref[...]
    ones_bd = ones_ref[...]
    o_ka, o_va, o_qb, o_kb, o_vb = WIDTH_A, 2 * WIDTH_A, 3 * WIDTH_A, 3 * WIDTH_A + WIDTH_B, 3 * WIDTH_A + WIDTH_B + KV_WIDTH_B

    p = _dot(h, w_ref[:, 0:o_ka])
    for c in range(WIDTH_A // LANES):
        stage_ref[c] = _rope(p[:, c * LANES:(c + 1) * LANES], ca, sa1, sa2, ROT_DIM_A // 2) * (SCALE * LOG2E)
    _store_residue_views(stage_ref, qa_refs)
    p = _dot(h, w_ref[:, o_ka:o_va])
    for c in range(WIDTH_A // LANES):
        stage_ref[c] = _rope(p[:, c * LANES:(c + 1) * LANES], ca, sa1, sa2, ROT_DIM_A // 2)
    _store_residue_views(stage_ref, ka_refs)
    p = _dot(h, w_ref[:, o_va:o_qb])
    for c in range(WIDTH_A // LANES):
        stage_ref[c] = p[:, c * LANES:(c + 1) * LANES]
    _store_residue_views(stage_ref, va_refs)

    p = _dot(h, w_ref[:, o_qb:o_kb])
    qn = qn_ref[...]
    for c in range(WIDTH_B // LANES):
        sl = slice(c * LANES, (c + 1) * LANES)
        v = _head_norm(p[:, sl], qn, ones_bd)
        qb_ref[:, sl] = (_rope(v, cb, sb1, sb2, HEAD_DIM // 4) * (SCALE * LOG2E)).astype(BF16)
    p = _dot(h, w_ref[:, o_kb:o_vb])
    v = _head_norm(p, kn_ref[...], ones_bd)
    kb_ref[...] = _rope(v, cb, sb1, sb2, HEAD_DIM // 4).astype(BF16)
    vb = _dot(h, w_ref[:, o_vb:IN_WIDTH])
    lo_half = lax.broadcasted_iota(jnp.int32, vb.shape, 1) < HEAD_DIM
    vb_ref[:, 0:LANES] = jnp.where(lo_half, vb, 1.0).astype(BF16)
    vb_ref[:, LANES:2 * LANES] = jnp.where(lo_half, 1.0, vb).astype(BF16)


def _rope_tables(seq):
    lane = jnp.arange(LANES) % HEAD_DIM
    pos = jnp.arange(seq)

    def cos_sin(p, dim, theta):
        inv = theta ** (-jnp.arange(0, dim, 2, dtype=F32) / dim)
        ang = p.astype(F32)[:, None] * inv[None, :]
        return jnp.cos(ang), jnp.sin(ang)

    ha = ROT_DIM_A // 2
    cos_a, sin_a = cos_sin(pos, ROT_DIM_A, ROPE_THETA_A)
    fa = lane % ha
    in_lo, in_hi = lane < ha, (lane >= ha) & (lane < ROT_DIM_A)
    ca = jnp.where((in_lo | in_hi)[None, :], cos_a[:, fa], 1.0)
    sa1 = jnp.where(in_lo[None, :], -sin_a[:, fa], 0.0)
    sa2 = jnp.where(in_hi[None, :], sin_a[:, fa], 0.0)
    hb = HEAD_DIM // 4
    cos_r, sin_r = cos_sin(pos // GRID_W, HEAD_DIM // 2, ROPE_THETA_B)
    cos_c, sin_c = cos_sin(pos % GRID_W, HEAD_DIM // 2, ROPE_THETA_B)
    fb = lane % hb
    is_row = (lane < HEAD_DIM // 2)[None, :]
    cosb = jnp.where(is_row, cos_r[:, fb], cos_c[:, fb])
    sinb = jnp.where(is_row, sin_r[:, fb], sin_c[:, fb])
    first = ((lane // hb) % 2 == 0)[None, :]
    sb1 = jnp.where(first, -sinb, 0.0)
    sb2 = jnp.where(first, 0.0, sinb)
    return ca, sa1, sa2, cosb, sb1, sb2


def _project(x2d, seq, norm_mix, w_in_bf, q_norm, k_norm, tables):
    n = x2d.shape[0]
    tm = ROW_TILE
    tiles_per_seq = seq // tm
    row = lambda i: (i, 0)
    fixed = lambda i: (0, 0)
    tab = lambda i: (i % tiles_per_seq, 0)
    ones_bd = (jnp.arange(LANES)[:, None] // HEAD_DIM == jnp.arange(LANES)[None, :] // HEAD_DIM).astype(BF16)
    gain2 = lambda g: jnp.tile(g.reshape(1, HEAD_DIM), (1, LANES // HEAD_DIM))
    tab_spec = pl.BlockSpec((tm, LANES), tab)
    batch = n // seq
    view_shapes = [jax.ShapeDtypeStruct((batch, seq // d, d * WIDTH_A), BF16) for d in DILATIONS] * 3
    view_specs = [pl.BlockSpec((1, tm // d, d * WIDTH_A), lambda i: (i // tiles_per_seq, i % tiles_per_seq, 0))
                  for d in DILATIONS] * 3
    out_shapes = view_shapes + [
        jax.ShapeDtypeStruct((n, WIDTH_B), BF16), jax.ShapeDtypeStruct((n, KV_WIDTH_B), BF16),
        jax.ShapeDtypeStruct((n, 2 * KV_WIDTH_B), BF16)]
    outs = pl.pallas_call(
        _proj_kernel,
        grid=(n // tm,),
        in_specs=[pl.BlockSpec((tm, D_MODEL), row), pl.BlockSpec((1, D_MODEL), fixed),
                  pl.BlockSpec((D_MODEL, IN_WIDTH), fixed), pl.BlockSpec((1, LANES), fixed),
                  pl.BlockSpec((1, LANES), fixed)] + [tab_spec] * 6 + [pl.BlockSpec((LANES, LANES), fixed)],
        out_specs=view_specs + [pl.BlockSpec((tm, WIDTH_B), row), pl.BlockSpec((tm, KV_WIDTH_B), row),
                                pl.BlockSpec((tm, 2 * KV_WIDTH_B), row)],
        out_shape=out_shapes,
        scratch_shapes=[pltpu.VMEM((WIDTH_A // LANES, tm, LANES), F32)],
        name="in_proj",
        compiler_params=_cparams(("parallel",)),
    )(x2d, norm_mix.reshape(1, D_MODEL), w_in_bf, gain2(q_norm), gain2(k_norm), *tables, ones_bd)
    nd = len(DILATIONS)
    return outs[0:nd], outs[nd:2 * nd], outs[2 * nd:3 * nd], outs[3 * nd], outs[3 * nd + 1], outs[3 * nd + 2]


def _band_kernel(q_ref, kp_ref, kc_ref, kn_ref, vp_ref, vc_ref, vn_ref, o_ref, lse_ref, *, sub_len):
    i = pl.program_id(2)
    t = Q_TILE
    nk = t + 2 * BAND_HALF
    kall = jnp.concatenate([kp_ref[0, t - BAND_HALF:t], kc_ref[0], kn_ref[0, 0:BAND_HALF]], axis=0)
    vall = jnp.concatenate([vp_ref[0, t - BAND_HALF:t], vc_ref[0], vn_ref[0, 0:BAND_HALF]], axis=0)
    row = lax.broadcasted_iota(jnp.int32, (t, nk), 0)
    col = lax.broadcasted_iota(jnp.int32, (t, nk), 1)
    rel = col - BAND_HALF - row
    in_band = (rel >= -BAND_HALF) & (rel <= BAND_HALF)
    lo_half = lax.broadcasted_iota(jnp.int32, (t, LANES), 1) < HEAD_DIM
    zero = jnp.zeros((), BF16)
    for u in range(BAND_STEP // t):
        rows = slice(u * t, (u + 1) * t)
        kj = i * BAND_STEP + u * t - BAND_HALF + col
        valid = in_band & (kj >= 0) & (kj < sub_len)
        q, kw, vw = q_ref[0, rows, :], kall[u * t:u * t + nk], vall[u * t:u * t + nk]
        for c in range(WIDTH_A // LANES):
            sl = slice(c * LANES, (c + 1) * LANES)
            qc, kc, vc = q[:, sl], kw[:, sl], vw[:, sl]
            outs, lses = [], []
            for half_mask in (lo_half, jnp.logical_not(lo_half)):
                s = _dot_nt(jnp.where(half_mask, qc, zero), kc)
                s = jnp.where(valid, s, NEG_INF)
                m = jnp.max(s, axis=-1, keepdims=True)
                p = jnp.exp2(s - m)
                l = jnp.sum(p, axis=-1, keepdims=True)
                outs.append(_dot(p.astype(BF16), vc) / l)
                lses.append(jnp.broadcast_to(m * LN2 + jnp.log(l), (t, LANES)))
            o_ref[0, rows, sl] = jnp.where(lo_half, outs[0], outs[1])
            lse_ref[0, rows, sl] = jnp.where(lo_half, lses[0], lses[1])


def _banded_attention(q, k, v, batch, seq, dil):
    sub_len = seq // dil
    nb = sub_len // BAND_STEP
    per = BAND_STEP // Q_TILE
    halo = (1, Q_TILE, WIDTH_A)
    cur = pl.BlockSpec((1, BAND_STEP, WIDTH_A), lambda b, r, i: (b, i, r))
    prev = pl.BlockSpec(halo, lambda b, r, i: (b, jnp.maximum(per * i - 1, 0), r))
    nxt = pl.BlockSpec(halo, lambda b, r, i: (b, jnp.minimum(per * (i + 1), per * nb - 1), r))
    shape = jax.ShapeDtypeStruct((batch, sub_len, dil * WIDTH_A), F32)
    o, lse = pl.pallas_call(
        functools.partial(_band_kernel, sub_len=sub_len),
        grid=(batch, dil, nb),
        in_specs=[cur, prev, cur, nxt, prev, cur, nxt],
        out_specs=[cur, cur],
        out_shape=(shape, shape),
        name=f"band_attn_d{dil}",
        compiler_params=_cparams(("parallel", "parallel", "parallel")),
    )(q, k, k, k, v, v, v)
    return o, lse


def _gqa_kernel(q_ref, k_ref, v_ref, o_ref, qs_ref, m_ref, acc_ref, s0_ref, s1_ref, *, seq):
    tq = QB_TILE
    group = N_HEADS_B // N_KV_B
    lane_half = lax.broadcasted_iota(jnp.int32, (tq, LANES), 1) // HEAD_DIM
    heads_out = [None] * N_HEADS_B
    for n in range(N_KV_B):
        for g in range(group):
            hd = n * group + g
            qc = q_ref[0, :, (hd // 2) * LANES:(hd // 2 + 1) * LANES].astype(F32)
            if hd % 2 != n:
                qc = pltpu.roll(qc, HEAD_DIM, 1)
            qs_ref[g * tq:(g + 1) * tq, :] = jnp.where(lane_half == n, qc, 0.0).astype(BF16)
        m_ref[...] = jnp.full(m_ref.shape, -jnp.inf, F32)
        acc_ref[...] = jnp.zeros(acc_ref.shape, F32)

        def scores(j, dst_ref):
            start = pl.multiple_of(j * KB_TILE, KB_TILE)
            dst_ref[...] = _dot_nt(qs_ref[...], k_ref[0, pl.ds(start, KB_TILE), :])

        def accumulate(j, src_ref):
            start = pl.multiple_of(j * KB_TILE, KB_TILE)
            vc = v_ref[0, pl.ds(start, KB_TILE), n * LANES:(n + 1) * LANES]
            s = src_ref[...]
            m_old = m_ref[...]
            m_new = jnp.maximum(m_old, jnp.max(s, axis=-1, keepdims=True))
            p = jnp.exp2(s - jnp.concatenate([m_new] * (KB_TILE // LANES), axis=1))
            acc_ref[...] = jnp.exp2(m_old - m_new) * acc_ref[...] + _dot(p.astype(BF16), vc)
            m_ref[...] = m_new

        def kv_pair(i, carry):
            j = 2 * i
            scores(j + 1, s1_ref)
            accumulate(j, s0_ref)
            scores(j + 2, s0_ref)
            accumulate(j + 1, s1_ref)
            return carry

        n_chunks = seq // KB_TILE
        scores(0, s0_ref)
        lax.fori_loop(0, n_chunks // 2 - 1, kv_pair, 0)
        scores(n_chunks - 1, s1_ref)
        accumulate(n_chunks - 2, s0_ref)
        accumulate(n_chunks - 1, s1_ref)
        acc = acc_ref[...]
        res = acc / pltpu.roll(acc, HEAD_DIM, 1)
        for g in range(group):
            heads_out[n * group + g] = res[g * tq:(g + 1) * tq, :]
    for c in range(WIDTH_B // LANES):
        even, odd = heads_out[2 * c], heads_out[2 * c + 1]
        n = (2 * c) // group
        if n == 0:
            odd = pltpu.roll(odd, HEAD_DIM, 1)
        else:
            even = pltpu.roll(even, HEAD_DIM, 1)
        o_ref[0, :, c * LANES:(c + 1) * LANES] = jnp.where(lane_half == 0, even, odd)


def _gqa_attention(q, k, v, batch, seq):
    tq = QB_TILE
    rows = (N_HEADS_B // N_KV_B) * tq
    o = pl.pallas_call(
        functools.partial(_gqa_kernel, seq=seq),
        grid=(batch, seq // tq),
        in_specs=[pl.BlockSpec((1, tq, WIDTH_B), lambda b, i: (b, i, 0)),
                  pl.BlockSpec((1, seq, KV_WIDTH_B), lambda b, i: (b, 0, 0)),
                  pl.BlockSpec((1, seq, 2 * KV_WIDTH_B), lambda b, i: (b, 0, 0))],
        out_specs=pl.BlockSpec((1, tq, WIDTH_B), lambda b, i: (b, i, 0)),
        out_shape=jax.ShapeDtypeStruct((batch, seq, WIDTH_B), F32),
        scratch_shapes=[pltpu.VMEM((rows, LANES), BF16), pltpu.VMEM((rows, LANES), F32),
                        pltpu.VMEM((rows, LANES), F32), pltpu.VMEM((rows, KB_TILE), F32),
                        pltpu.VMEM((rows, KB_TILE), F32)],
        name="gqa_attn",
        compiler_params=_cparams(("parallel", "parallel")),
    )(q.reshape(batch, seq, WIDTH_B), k.reshape(batch, seq, KV_WIDTH_B), v.reshape(batch, seq, 2 * KV_WIDTH_B))
    return o.reshape(batch * seq, WIDTH_B)


def _load_residue_view(view_ref, dil, stage_ref):
    if dil == 1:
        return view_ref[0]
    cols, rows, _ = stage_ref.shape
    for r in range(dil):
        for c in range(cols):
            lane0 = (r * cols + c) * LANES
            stage_ref[c, pl.ds(r, rows // dil, stride=dil), :] = view_ref[0, :, lane0:lane0 + LANES]
    return jnp.concatenate([stage_ref[c] for c in range(cols)], axis=1)


def _post_kernel(*refs):
    nd = len(DILATIONS)
    o_refs, l_refs = refs[0:nd], refs[nd:2 * nd]
    ob_ref, x_ref, ga_ref, gb_ref, wo_ref, gf_ref, wr_ref, x1_ref, h2_ref, afft_ref = refs[2 * nd:2 * nd + 10]
    stages = refs[2 * nd + 10:]
    lses = [_load_residue_view(ref, d, stages[2 * i]) for i, (ref, d) in enumerate(zip(l_refs, DILATIONS))]
    outs = [_load_residue_view(ref, d, stages[2 * i + 1]) for i, (ref, d) in enumerate(zip(o_refs, DILATIONS))]
    mx = functools.reduce(jnp.maximum, lses)
    ws = [jnp.exp(l - mx) for l in lses]
    add = lambda a, b: a + b
    oa = functools.reduce(add, [w * o for w, o in zip(ws, outs)]) / functools.reduce(add, ws)
    na = _rms(oa, ga_ref[...]).astype(BF16)
    nb = _rms(ob_ref[...], gb_ref[...]).astype(BF16)
    mixed = _dot(na, wo_ref[0:WIDTH_A, :]) + _dot(nb, wo_ref[WIDTH_A:WIDTH_A + WIDTH_B, :])
    x1 = x_ref[...] + mixed
    x1_ref[0] = x1
    x1_ref[1] = jnp.zeros(x1.shape, F32)
    h2 = _rms(x1, gf_ref[...])
    logits = _dot(h2.astype(BF16), wr_ref[...])
    lane = lax.broadcasted_iota(jnp.int32, logits.shape, 1)
    logits = jnp.where(lane < N_EXPERTS, logits, -jnp.inf)
    ex = jnp.exp(logits - jnp.max(logits, axis=-1, keepdims=True))
    aff = ex / jnp.sum(ex, axis=-1, keepdims=True)
    h2_ref[:, 0:D_MODEL] = h2
    h2_ref[:, D_MODEL:D_MODEL + AFF_PAD] = aff
    afft_ref[...] = aff.T[0:N_EXPERTS, :]


def _post_attention(oas, lses, ob, x2d, seq, acc_rows, norm_out_a, norm_out_b, w_out_bf, norm_ffn, w_router_bf):
    n = x2d.shape[0]
    tm = ROW_TILE
    tiles_per_seq = seq // tm
    n_tiles = n // tm
    extra = -(-(acc_rows - n) // tm)
    tile = lambda i: jnp.minimum(i, n_tiles - 1)
    row = lambda i: (tile(i), 0)
    fixed = lambda i: (0, 0)
    views = [pl.BlockSpec((1, tm // d, d * WIDTH_A), lambda i: (tile(i) // tiles_per_seq, tile(i) % tiles_per_seq, 0))
             for d in DILATIONS]
    wr_pad = jnp.zeros((D_MODEL, AFF_PAD), BF16).at[:, :N_EXPERTS].set(w_router_bf)
    return pl.pallas_call(
        _post_kernel,
        grid=(n_tiles + extra,),
        in_specs=views + views + [pl.BlockSpec((tm, WIDTH_B), row), pl.BlockSpec((tm, D_MODEL), row),
                                  pl.BlockSpec((1, WIDTH_A), fixed), pl.BlockSpec((1, WIDTH_B), fixed),
                                  pl.BlockSpec((WIDTH_A + WIDTH_B, D_MODEL), fixed),
                                  pl.BlockSpec((1, D_MODEL), fixed), pl.BlockSpec((D_MODEL, AFF_PAD), fixed)],
        out_specs=[pl.BlockSpec((2, tm, D_MODEL), lambda i: (0, i, 0)), pl.BlockSpec((tm, D_MODEL + AFF_PAD), row),
                   pl.BlockSpec((N_EXPERTS, tm), lambda i: (0, tile(i)))],
        out_shape=(jax.ShapeDtypeStruct((2, acc_rows, D_MODEL), F32), jax.ShapeDtypeStruct((n, D_MODEL + AFF_PAD), F32),
                   jax.ShapeDtypeStruct((N_EXPERTS, n), F32)),
        scratch_shapes=[pltpu.VMEM((WIDTH_A // LANES, tm, LANES), F32)] * (2 * len(DILATIONS)),
        name="post_attn",
        compiler_params=_cparams(("arbitrary",)),
    )(*oas, *lses, ob, x2d, norm_out_a.reshape(1, WIDTH_A), norm_out_b.reshape(1, WIDTH_B), w_out_bf,
      norm_ffn.reshape(1, D_MODEL), wr_pad)


def _select_kernel(aff_ref, idx_ref, *, cap):
    ne, nch, _ = aff_ref.shape
    aff = aff_ref[...]

    def count(mask):
        return jnp.sum(jnp.sum(mask.astype(F32), axis=1, keepdims=True), axis=2, keepdims=True)

    def search(i, thr_bits):
        cand = thr_bits | jnp.left_shift(jnp.int32(1), 30 - i)
        return jnp.where(count(aff >= lax.bitcast_convert_type(cand, F32)) >= cap, cand, thr_bits)

    thr = lax.bitcast_convert_type(lax.fori_loop(0, 31, search, jnp.zeros((ne, 1, 1), jnp.int32)), F32)

    ci = lax.broadcasted_iota(jnp.int32, (LANES, LANES), 0)
    cj = lax.broadcasted_iota(jnp.int32, (LANES, LANES), 1)
    upper = (ci <= cj).astype(BF16)
    ones = jnp.ones((LANES, LANES), BF16)
    ri = lax.broadcasted_iota(jnp.int32, (nch, nch), 0)
    rj = lax.broadcasted_iota(jnp.int32, (nch, nch), 1)
    strict_lower = (rj < ri).astype(BF16)

    def prefix(mask_f):
        mb = mask_f.astype(BF16).reshape(ne * nch, LANES)
        local = _dot(mb, upper).reshape(ne, nch, LANES)
        tot = _dot(mb, ones).reshape(ne, nch, LANES)
        offs = jnp.stack([_dot(strict_lower, tot[e].astype(BF16)) for e in range(ne)])
        return local, tot, offs

    gt = aff > thr
    eq = aff == thr
    need = cap - count(gt)
    eq_local, _, eq_offs = prefix(eq.astype(F32))
    sel = jnp.logical_or(gt, jnp.logical_and(eq, (eq_local + eq_offs) <= need)).astype(F32)
    loc, tot, offs = prefix(sel)

    slot = lax.broadcasted_iota(jnp.int32, (1, cap), 1).astype(F32)
    chunk_id = lax.broadcasted_iota(jnp.int32, (nch, 1), 0).astype(F32)
    for e in range(ne):
        before = offs[e][:, 0:1]
        through = before + tot[e][:, 0:1]
        chunk_of = jnp.sum((through <= slot).astype(F32), axis=0, keepdims=True)
        onehot = chunk_id == chunk_of
        base = jnp.sum(jnp.where(onehot, before, 0.0), axis=0, keepdims=True)
        rows = _dot(loc[e].T.astype(BF16), onehot.astype(BF16))
        lane_of = jnp.sum((rows <= slot - base).astype(F32), axis=0, keepdims=True)
        idx_ref[e:e + 1, :] = (chunk_of * LANES + lane_of).astype(jnp.int32)


def _select(aff_t, cap):
    ne, n = aff_t.shape
    return pl.pallas_call(
        functools.partial(_select_kernel, cap=cap),
        out_shape=jax.ShapeDtypeStruct((ne, cap), jnp.int32),
        name="ec_select",
        compiler_params=pltpu.CompilerParams(vmem_limit_bytes=VMEM_LIMIT),
    )(aff_t.reshape(ne, n // LANES, LANES))


def _expert_kernel(idx_ref, idxp_ref, idxn_ref, h2_hbm, wg_ref, wu_ref, wd_ref, acc_in_hbm, out_hbm,
                   xbuf, xe, gate, acc, gbuf, sbuf, sem_x, sem_g, sem_s, *, cap, chunk, n_tok):
    del acc_in_hbm
    e = pl.program_id(0)
    f = pl.program_id(1)
    n_exp = pl.num_programs(0)
    last = pl.num_programs(1) - 1
    capp = xbuf.shape[0]
    par = e % 2

    def x_copy(tok, j):
        return pltpu.make_async_copy(h2_hbm.at[pl.ds(jnp.minimum(tok, n_tok - 1), 1), :], xbuf.at[pl.ds(j, 1), :],
                                     sem_x)

    def g_copy(tok, j):
        return pltpu.make_async_copy(out_hbm.at[par, pl.ds(tok, 1), :], gbuf.at[pl.ds(j, 1), :], sem_g)

    def s_copy(tok, j, parity):
        return pltpu.make_async_copy(sbuf.at[pl.ds(j, 1), :], out_hbm.at[parity, pl.ds(tok, 1), :], sem_s)

    def all_rows(issue):
        def body(jo, carry):
            for u in range(8):
                issue(jo * 8 + u)
            return carry
        lax.fori_loop(0, capp // 8, body, 0)

    @pl.when(jnp.logical_and(e == 0, f == 0))
    def _first_expert():
        all_rows(lambda j: x_copy(idx_ref[0, 0, j], j).start())
        sbuf[...] = jnp.zeros(sbuf.shape, F32)

    @pl.when(f == 0)
    def _start_expert():
        pltpu.make_async_copy(h2_hbm.at[pl.ds(0, capp), :], xbuf, sem_x).wait()
        xe[...] = xbuf[0:cap, 0:D_MODEL].astype(BF16)
        aff = xbuf[0:cap, D_MODEL:D_MODEL + AFF_PAD]
        lane = lax.broadcasted_iota(jnp.int32, aff.shape, 1)
        gate[...] = jnp.sum(jnp.where(lane == e, aff, 0.0), axis=-1, keepdims=True)
        acc[...] = jnp.zeros(acc.shape, F32)

    base = f * chunk
    for u in range(chunk):
        j = base + u
        x_copy(idxn_ref[0, 0, j], j).start()
        g_copy(idx_ref[0, 0, j], j).start()
        s_copy(idxp_ref[0, 0, j], j, 1 - par).start()

    wg = wg_ref[0].astype(BF16)
    wu = wu_ref[0].astype(BF16)
    wd = wd_ref[0].astype(BF16)
    rc = min(EXPERT_ROW_CHUNK, cap)
    for r in range(cap // rc):
        rows = slice(r * rc, (r + 1) * rc)
        x = xe[rows, :]
        g = _dot(x, wg)
        hid = (g * (1.0 / (1.0 + jnp.exp(-g))) * _dot(x, wu)).astype(BF16)
        acc[rows, :] += _dot(hid, wd)

    @pl.when(f == last)
    def _finish_expert():
        pltpu.make_async_copy(out_hbm.at[0, pl.ds(0, capp), :], gbuf, sem_g).wait()
        pltpu.make_async_copy(sbuf, out_hbm.at[0, pl.ds(0, capp), :], sem_s).wait()
        sbuf[0:cap, :] = gbuf[0:cap, :] + acc[...] * gate[...]
        sbuf[cap:capp, :] = jnp.zeros((capp - cap, D_MODEL), F32)

    @pl.when(jnp.logical_and(f == last, e == n_exp - 1))
    def _last_expert():
        all_rows(lambda j: s_copy(idx_ref[0, 0, j], j, par).start())
        pltpu.make_async_copy(sbuf, out_hbm.at[0, pl.ds(0, capp), :], sem_s).wait()
        pltpu.make_async_copy(h2_hbm.at[pl.ds(0, capp), :], xbuf, sem_x).wait()


def _expert_chunking(cap):
    nf = EXPERT_FF // FF_TILE
    chunk = -(-cap // (nf * 8)) * 8
    return chunk, chunk * nf


def _experts(idx, h2_ext, acc2, n, w_gate, w_up, w_down, cap):
    nf = EXPERT_FF // FF_TILE
    chunk, capp = _expert_chunking(cap)
    pad = capp - cap
    idx_pad = jnp.concatenate([idx, jnp.broadcast_to(n + jnp.arange(pad, dtype=jnp.int32), (N_EXPERTS, pad))], axis=1)
    idx_pad = idx_pad.reshape(N_EXPERTS, 1, capp)
    idx_spec = lambda fn: pl.BlockSpec((1, 1, capp), fn, memory_space=pltpu.SMEM)
    return pl.pallas_call(
        functools.partial(_expert_kernel, cap=cap, chunk=chunk, n_tok=n),
        grid=(N_EXPERTS, nf),
        in_specs=[idx_spec(lambda e, f: (e, 0, 0)),
                  idx_spec(lambda e, f: (jnp.maximum(e - 1, 0), 0, 0)),
                  idx_spec(lambda e, f: (jnp.minimum(e + 1, N_EXPERTS - 1), 0, 0)),
                  pl.BlockSpec(memory_space=pl.ANY),
                  pl.BlockSpec((1, D_MODEL, FF_TILE), lambda e, f: (e, 0, f)),
                  pl.BlockSpec((1, D_MODEL, FF_TILE), lambda e, f: (e, 0, f)),
                  pl.BlockSpec((1, FF_TILE, D_MODEL), lambda e, f: (e, f, 0)),
                  pl.BlockSpec(memory_space=pl.ANY)],
        out_specs=pl.BlockSpec(memory_space=pl.ANY),
        out_shape=jax.ShapeDtypeStruct((2, n + pad, D_MODEL), F32),
        scratch_shapes=[pltpu.VMEM((capp, D_MODEL + AFF_PAD), F32), pltpu.VMEM((cap, D_MODEL), BF16),
                        pltpu.VMEM((cap, 1), F32), pltpu.VMEM((cap, D_MODEL), F32),
                        pltpu.VMEM((capp, D_MODEL), F32), pltpu.VMEM((capp, D_MODEL), F32),
                        pltpu.SemaphoreType.DMA(()), pltpu.SemaphoreType.DMA(()), pltpu.SemaphoreType.DMA(())],
        input_output_aliases={7: 0},
        name="experts",
        compiler_params=pltpu.CompilerParams(dimension_semantics=("arbitrary", "arbitrary"),
                                             vmem_limit_bytes=EXPERT_VMEM_LIMIT),
    )(idx_pad, idx_pad, idx_pad, h2_ext, w_gate, w_up, w_down, acc2)


def _final_kernel(a_ref, b_ref, g_ref, o_ref):
    o_ref[...] = _rms(a_ref[0] + b_ref[0], g_ref[...])


def _final_norm(acc2, n, gain):
    tm = ROW_TILE
    return pl.pallas_call(
        _final_kernel,
        grid=(n // tm,),
        in_specs=[pl.BlockSpec((1, tm, D_MODEL), lambda i: (0, i, 0)),
                  pl.BlockSpec((1, tm, D_MODEL), lambda i: (1, i, 0)),
                  pl.BlockSpec((1, D_MODEL), lambda i: (0, 0))],
        out_specs=pl.BlockSpec((tm, D_MODEL), lambda i: (i, 0)),
        out_shape=jax.ShapeDtypeStruct((n, D_MODEL), F32),
        name="final_norm",
        compiler_params=_cparams(("parallel",)),
    )(acc2, acc2, gain.reshape(1, D_MODEL))


def _trunk(x, norm_mix, w_in, q_norm_b, k_norm_b, norm_out_a, norm_out_b, w_out, norm_ffn, w_router,
           w_gate, w_up, w_down, norm_final):
    batch, seq, _ = x.shape
    n = batch * seq
    cap = EC_CAPACITY_FACTOR * n // N_EXPERTS
    x2d = x.reshape(n, D_MODEL)
    tables = _rope_tables(seq)
    assert norm_mix.shape[0] == 1, "single trunk layer"
    l = 0
    qa, ka, va, qb, kb, vb = _project(x2d, seq, norm_mix[l], w_in[l].astype(BF16), q_norm_b[l], k_norm_b[l], tables)
    pats = [_banded_attention(q, k, v, batch, seq, d) for q, k, v, d in zip(qa, ka, va, DILATIONS)]
    ob = _gqa_attention(qb, kb, vb, batch, seq)
    acc_rows = n + _expert_chunking(cap)[1] - cap
    acc2, h2_ext, aff_t = _post_attention([p[0] for p in pats], [p[1] for p in pats], ob, x2d, seq, acc_rows,
                                          norm_out_a[l], norm_out_b[l], w_out[l].astype(BF16), norm_ffn[l],
                                          w_router[l].astype(BF16))
    idx = _select(aff_t, cap)
    acc2 = _experts(idx, h2_ext, acc2, n, w_gate[l], w_up[l], w_down[l], cap)
    return _final_norm(acc2, n, norm_final).reshape(batch, seq, D_MODEL)


def kernel(x_prompt, x_sample, norm_mix, w_in, q_norm_b, k_norm_b, norm_out_a, norm_out_b, w_out, norm_ffn,
           w_router, w_gate, w_up, w_down, norm_final):
    params = (norm_mix, w_in, q_norm_b, k_norm_b, norm_out_a, norm_out_b, w_out, norm_ffn, w_router, w_gate,
              w_up, w_down, norm_final)
    return (_trunk(x_prompt, *params), _trunk(x_sample, *params))
```

```python
import functools

import jax
import jax.numpy as jnp
from jax import lax
from jax.experimental import pallas as pl
from jax.experimental.pallas import tpu as pltpu

F32 = jnp.float32
BF16 = jnp.bfloat16

D_MODEL = 1024
HEAD_DIM = 64
N_HEADS_A = 8
N_HEADS_B = 8
N_KV_B = 2
WIDTH_A = N_HEADS_A * HEAD_DIM
WIDTH_B = N_HEADS_B * HEAD_DIM
KV_WIDTH_B = N_KV_B * HEAD_DIM
IN_WIDTH = 3 * WIDTH_A + WIDTH_B + 2 * KV_WIDTH_B
DILATIONS = (1, 4, 16)
BAND_HALF = 64
ROPE_THETA_A = 500000.0
ROT_DIM_A = HEAD_DIM // 4
ROPE_THETA_B = 10000.0
GRID_W = 64
N_EXPERTS = 16
EC_CAPACITY_FACTOR = 2
EXPERT_FF = 2816
EPS = 1e-6
NEG_INF = -1e30
SCALE = HEAD_DIM ** -0.5
LOG2E = 1.4426950408889634
LN2 = 0.6931471805599453

LANES = 128
ROW_TILE = 256
Q_TILE = 128
BAND_STEP = 256
QB_TILE = 256
KB_TILE = 1024
FF_TILE = 256
AFF_PAD = LANES
EXPERT_ROW_CHUNK = 512
VMEM_LIMIT = 56 * 1024 * 1024
EXPERT_VMEM_LIMIT = 60 * 1024 * 1024


def _cparams(sem):
    return pltpu.CompilerParams(dimension_semantics=sem, vmem_limit_bytes=VMEM_LIMIT)


def _rms(x, g):
    return x * lax.rsqrt(jnp.mean(x * x, axis=-1, keepdims=True) + EPS) * g


def _dot(a, b):
    return jnp.dot(a, b, preferred_element_type=F32)


def _dot_nt(a, b):
    return lax.dot_general(a, b, (((1,), (1,)), ((), ())), preferred_element_type=F32)


def _rope(v, c, s_next, s_prev, shift):
    return v * c + pltpu.roll(v, LANES - shift, 1) * s_next + pltpu.roll(v, shift, 1) * s_prev


def _head_norm(v, gain, ones_bd):
    sq = v * v
    hi = sq.astype(BF16)
    lo = (sq - hi.astype(F32)).astype(BF16)
    ss = _dot(hi, ones_bd) + _dot(lo, ones_bd)
    return v * lax.rsqrt(ss * (1.0 / HEAD_DIM) + EPS) * gain


def _store_residue_views(stage_ref, view_refs):
    cols, rows, _ = stage_ref.shape
    for dil, ref in zip(DILATIONS, view_refs):
        for r in range(dil):
            for c in range(cols):
                part = stage_ref[c] if dil == 1 else stage_ref[c, pl.ds(r, rows // dil, stride=dil), :]
                lane0 = (r * cols + c) * LANES
                ref[0, :, lane0:lane0 + LANES] = part.astype(BF16)


def _proj_kernel(x_ref, g_ref, w_ref, qn_ref, kn_ref, ca_ref, sa1_ref, sa2_ref, cb_ref, sb1_ref, sb2_ref,
                 ones_ref, *refs):
    nd = len(DILATIONS)
    qa_refs, ka_refs, va_refs = refs[0:nd], refs[nd:2 * nd], refs[2 * nd:3 * nd]
    qb_ref, kb_ref, vb_ref, stage_ref = refs[3 * nd:]
    h = _rms(x_ref[...], g_ref[...]).astype(BF16)
    ca, sa1, sa2 = ca_ref[...], sa1_ref[...], sa2_ref[...]
    cb, sb1, sb2 = cb_ref[...], sb1_ref[...], sb2_ref[...]
    ones_bd = ones_ref[...]
    o_ka, o_va, o_qb, o_kb, o_vb = WIDTH_A, 2 * WIDTH_A, 3 * WIDTH_A, 3 * WIDTH_A + WIDTH_B, 3 * WIDTH_A + WIDTH_B + KV_WIDTH_B

    p = _dot(h, w_ref[:, 0:o_ka])
    for c in range(WIDTH_A // LANES):
        stage_ref[c] = _rope(p[:, c * LANES:(c + 1) * LANES], ca, sa1, sa2, ROT_DIM_A // 2) * (SCALE * LOG2E)
    _store_residue_views(stage_ref, qa_refs)
    p = _dot(h, w_ref[:, o_ka:o_va])
    for c in range(WIDTH_A // LANES):
        stage_ref[c] = _rope(p[:, c * LANES:(c + 1) * LANES], ca, sa1, sa2, ROT_DIM_A // 2)
    _store_residue_views(stage_ref, ka_refs)
    p = _dot(h, w_ref[:, o_va:o_qb])
    for c in range(WIDTH_A // LANES):
        stage_ref[c] = p[:, c * LANES:(c + 1) * LANES]
    _store_residue_views(stage_ref, va_refs)

    p = _dot(h, w_ref[:, o_qb:o_kb])
    qn = qn_ref[...]
    for c in range(WIDTH_B // LANES):
        sl = slice(c * LANES, (c + 1) * LANES)
        v = _head_norm(p[:, sl], qn, ones_bd)
        qb_ref[:, sl] = (_rope(v, cb, sb1, sb2, HEAD_DIM // 4) * (SCALE * LOG2E)).astype(BF16)
    p = _dot(h, w_ref[:, o_kb:o_vb])
    v = _head_norm(p, kn_ref[...], ones_bd)
    kb_ref[...] = _rope(v, cb, sb1, sb2, HEAD_DIM // 4).astype(BF16)
    vb = _dot(h, w_ref[:, o_vb:IN_WIDTH])
    lo_half = lax.broadcasted_iota(jnp.int32, vb.shape, 1) < HEAD_DIM
    vb_ref[0, 0, 0] = jnp.where(lo_half, vb, 1.0).T.astype(BF16)
    vb_ref[0, 1, 0] = jnp.where(lo_half, 1.0, vb).T.astype(BF16)


def _rope_tables(seq):
    lane = jnp.arange(LANES) % HEAD_DIM
    pos = jnp.arange(seq)

    def cos_sin(p, dim, theta):
        inv = theta ** (-jnp.arange(0, dim, 2, dtype=F32) / dim)
        ang = p.astype(F32)[:, None] * inv[None, :]
        return jnp.cos(ang), jnp.sin(ang)

    ha = ROT_DIM_A // 2
    cos_a, sin_a = cos_sin(pos, ROT_DIM_A, ROPE_THETA_A)
    fa = lane % ha
    in_lo, in_hi = lane < ha, (lane >= ha) & (lane < ROT_DIM_A)
    ca = jnp.where((in_lo | in_hi)[None, :], cos_a[:, fa], 1.0)
    sa1 = jnp.where(in_lo[None, :], -sin_a[:, fa], 0.0)
    sa2 = jnp.where(in_hi[None, :], sin_a[:, fa], 0.0)
    hb = HEAD_DIM // 4
    cos_r, sin_r = cos_sin(pos // GRID_W, HEAD_DIM // 2, ROPE_THETA_B)
    cos_c, sin_c = cos_sin(pos % GRID_W, HEAD_DIM // 2, ROPE_THETA_B)
    fb = lane % hb
    is_row = (lane < HEAD_DIM // 2)[None, :]
    cosb = jnp.where(is_row, cos_r[:, fb], cos_c[:, fb])
    sinb = jnp.where(is_row, sin_r[:, fb], sin_c[:, fb])
    first = ((lane // hb) % 2 == 0)[None, :]
    sb1 = jnp.where(first, -sinb, 0.0)
    sb2 = jnp.where(first, 0.0, sinb)
    return ca, sa1, sa2, cosb, sb1, sb2


def _project(x2d, seq, norm_mix, w_in_bf, q_norm, k_norm, tables):
    n = x2d.shape[0]
    tm = ROW_TILE
    tiles_per_seq = seq // tm
    tiles_per_chunk = KB_TILE // tm
    row = lambda i: (i, 0)
    fixed = lambda i: (0, 0)
    tab = lambda i: (i % tiles_per_seq, 0)
    ones_bd = (jnp.arange(LANES)[:, None] // HEAD_DIM == jnp.arange(LANES)[None, :] // HEAD_DIM).astype(BF16)
    gain2 = lambda g: jnp.tile(g.reshape(1, HEAD_DIM), (1, LANES // HEAD_DIM))
    tab_spec = pl.BlockSpec((tm, LANES), tab)
    batch = n // seq
    view_shapes = [jax.ShapeDtypeStruct((batch, seq // d, d * WIDTH_A), BF16) for d in DILATIONS] * 3
    view_specs = [pl.BlockSpec((1, tm // d, d * WIDTH_A), lambda i: (i // tiles_per_seq, i % tiles_per_seq, 0))
                  for d in DILATIONS] * 3
    out_shapes = view_shapes + [
        jax.ShapeDtypeStruct((n, WIDTH_B), BF16), jax.ShapeDtypeStruct((n, KV_WIDTH_B), BF16),
        jax.ShapeDtypeStruct((batch, N_KV_B, seq // KB_TILE, LANES, KB_TILE), BF16)]
    outs = pl.pallas_call(
        _proj_kernel,
        grid=(n // tm,),
        in_specs=[pl.BlockSpec((tm, D_MODEL), row), pl.BlockSpec((1, D_MODEL), fixed),
                  pl.BlockSpec((D_MODEL, IN_WIDTH), fixed), pl.BlockSpec((1, LANES), fixed),
                  pl.BlockSpec((1, LANES), fixed)] + [tab_spec] * 6 + [pl.BlockSpec((LANES, LANES), fixed)],
        out_specs=view_specs + [pl.BlockSpec((tm, WIDTH_B), row), pl.BlockSpec((tm, KV_WIDTH_B), row),
                                pl.BlockSpec((1, N_KV_B, 1, LANES, tm),
                                             lambda i: (i // tiles_per_seq, 0, (i % tiles_per_seq) // tiles_per_chunk,
                                                        0, i % tiles_per_chunk))],
        out_shape=out_shapes,
        scratch_shapes=[pltpu.VMEM((WIDTH_A // LANES, tm, LANES), F32)],
        name="in_proj",
        compiler_params=_cparams(("parallel",)),
    )(x2d, norm_mix.reshape(1, D_MODEL), w_in_bf, gain2(q_norm), gain2(k_norm), *tables, ones_bd)
    nd = len(DILATIONS)
    return outs[0:nd], outs[nd:2 * nd], outs[2 * nd:3 * nd], outs[3 * nd], outs[3 * nd + 1], outs[3 * nd + 2]


def _band_kernel(q_ref, kp_ref, kc_ref, kn_ref, vp_ref, vc_ref, vn_ref, o_ref, lse_ref, *, sub_len):
    i = pl.program_id(2)
    t = Q_TILE
    nk = t + 2 * BAND_HALF
    kall = jnp.concatenate([kp_ref[0, t - BAND_HALF:t], kc_ref[0], kn_ref[0, 0:BAND_HALF]], axis=0)
    vall = jnp.concatenate([vp_ref[0, t - BAND_HALF:t], vc_ref[0], vn_ref[0, 0:BAND_HALF]], axis=0)
    row = lax.broadcasted_iota(jnp.int32, (t, nk), 0)
    col = lax.broadcasted_iota(jnp.int32, (t, nk), 1)
    rel = col - BAND_HALF - row
    in_band = (rel >= -BAND_HALF) & (rel <= BAND_HALF)
    lo_half = lax.broadcasted_iota(jnp.int32, (t, LANES), 1) < HEAD_DIM
    zero = jnp.zeros((), BF16)
    for u in range(BAND_STEP // t):
        rows = slice(u * t, (u + 1) * t)
        kj = i * BAND_STEP + u * t - BAND_HALF + col
        valid = in_band & (kj >= 0) & (kj < sub_len)
        q, kw, vw = q_ref[0, rows, :], kall[u * t:u * t + nk], vall[u * t:u * t + nk]
        for c in range(WIDTH_A // LANES):
            sl = slice(c * LANES, (c + 1) * LANES)
            qc, kc, vc = q[:, sl], kw[:, sl], vw[:, sl]
            outs, lses = [], []
            for half_mask in (lo_half, jnp.logical_not(lo_half)):
                s = _dot_nt(jnp.where(half_mask, qc, zero), kc)
                s = jnp.where(valid, s, NEG_INF)
                m = jnp.max(s, axis=-1, keepdims=True)
                p = jnp.exp2(s - m)
                l = jnp.sum(p, axis=-1, keepdims=True)
                outs.append(_dot(p.astype(BF16), vc) / l)
                lses.append(jnp.broadcast_to(m * LN2 + jnp.log(l), (t, LANES)))
            o_ref[0, rows, sl] = jnp.where(lo_half, outs[0], outs[1])
            lse_ref[0, rows, sl] = jnp.where(lo_half, lses[0], lses[1])


def _banded_attention(q, k, v, batch, seq, dil):
    sub_len = seq // dil
    nb = sub_len // BAND_STEP
    per = BAND_STEP // Q_TILE
    halo = (1, Q_TILE, WIDTH_A)
    cur = pl.BlockSpec((1, BAND_STEP, WIDTH_A), lambda b, r, i: (b, i, r))
    prev = pl.BlockSpec(halo, lambda b, r, i: (b, jnp.maximum(per * i - 1, 0), r))
    nxt = pl.BlockSpec(halo, lambda b, r, i: (b, jnp.minimum(per * (i + 1), per * nb - 1), r))
    shape = jax.ShapeDtypeStruct((batch, sub_len, dil * WIDTH_A), F32)
    o, lse = pl.pallas_call(
        functools.partial(_band_kernel, sub_len=sub_len),
        grid=(batch, dil, nb),
        in_specs=[cur, prev, cur, nxt, prev, cur, nxt],
        out_specs=[cur, cur],
        out_shape=(shape, shape),
        name=f"band_attn_d{dil}",
        compiler_params=_cparams(("parallel", "parallel", "parallel")),
    )(q, k, k, k, v, v, v)
    return o, lse


def _gqa_kernel(q_ref, k_ref, vt_ref, o_ref, qst_ref, m_ref, acc_ref, s0_ref, s1_ref, *, seq):
    tq = QB_TILE
    group = N_HEADS_B // N_KV_B
    lane_half = lax.broadcasted_iota(jnp.int32, (tq, LANES), 1) // HEAD_DIM
    heads_out = [None] * N_HEADS_B
    for n in range(N_KV_B):
        for g in range(group):
            hd = n * group + g
            qc = q_ref[0, :, (hd // 2) * LANES:(hd // 2 + 1) * LANES].astype(F32)
            if hd % 2 != n:
                qc = pltpu.roll(qc, HEAD_DIM, 1)
            qst_ref[:, g * tq:(g + 1) * tq] = jnp.where(lane_half == n, qc, 0.0).T.astype(BF16)
        m_ref[...] = jnp.full(m_ref.shape, -jnp.inf, F32)
        acc_ref[...] = jnp.zeros(acc_ref.shape, F32)

        def scores(j, dst_ref):
            start = pl.multiple_of(j * KB_TILE, KB_TILE)
            dst_ref[...] = _dot(k_ref[0, pl.ds(start, KB_TILE), :], qst_ref[...])

        def accumulate(j, src_ref):
            st = src_ref[...]
            m_old = m_ref[...]
            m_new = jnp.maximum(m_old, jnp.max(st, axis=0, keepdims=True))
            pt = jnp.exp2(st - m_new).astype(BF16)
            acc_ref[...] = jnp.exp2(m_old - m_new) * acc_ref[...] + _dot(vt_ref[0, n, j], pt)
            m_ref[...] = m_new

        def kv_pair(i, carry):
            j = 2 * i
            scores(j + 1, s1_ref)
            accumulate(j, s0_ref)
            scores(j + 2, s0_ref)
            accumulate(j + 1, s1_ref)
            return carry

        n_chunks = seq // KB_TILE
        scores(0, s0_ref)
        lax.fori_loop(0, n_chunks // 2 - 1, kv_pair, 0)
        scores(n_chunks - 1, s1_ref)
        accumulate(n_chunks - 2, s0_ref)
        accumulate(n_chunks - 1, s1_ref)
        acc = acc_ref[...]
        res = acc / pltpu.roll(acc, HEAD_DIM, 0)
        for g in range(group):
            heads_out[n * group + g] = res[:, g * tq:(g + 1) * tq].T
    for c in range(WIDTH_B // LANES):
        even, odd = heads_out[2 * c], heads_out[2 * c + 1]
        n = (2 * c) // group
        if n == 0:
            odd = pltpu.roll(odd, HEAD_DIM, 1)
        else:
            even = pltpu.roll(even, HEAD_DIM, 1)
        o_ref[0, :, c * LANES:(c + 1) * LANES] = jnp.where(lane_half == 0, even, odd)


def _gqa_attention(q, k, vt, batch, seq):
    tq = QB_TILE
    rows = (N_HEADS_B // N_KV_B) * tq
    n_chunks = seq // KB_TILE
    o = pl.pallas_call(
        functools.partial(_gqa_kernel, seq=seq),
        grid=(batch, seq // tq),
        in_specs=[pl.BlockSpec((1, tq, WIDTH_B), lambda b, i: (b, i, 0)),
                  pl.BlockSpec((1, seq, KV_WIDTH_B), lambda b, i: (b, 0, 0)),
                  pl.BlockSpec((1, N_KV_B, n_chunks, LANES, KB_TILE), lambda b, i: (b, 0, 0, 0, 0))],
        out_specs=pl.BlockSpec((1, tq, WIDTH_B), lambda b, i: (b, i, 0)),
        out_shape=jax.ShapeDtypeStruct((batch, seq, WIDTH_B), F32),
        scratch_shapes=[pltpu.VMEM((LANES, rows), BF16), pltpu.VMEM((1, rows), F32),
                        pltpu.VMEM((LANES, rows), F32), pltpu.VMEM((KB_TILE, rows), F32),
                        pltpu.VMEM((KB_TILE, rows), F32)],
        name="gqa_attn",
        compiler_params=_cparams(("parallel", "parallel")),
    )(q.reshape(batch, seq, WIDTH_B), k.reshape(batch, seq, KV_WIDTH_B), vt)
    return o.reshape(batch * seq, WIDTH_B)


def _load_residue_view(view_ref, dil, stage_ref):
    if dil == 1:
        return view_ref[0]
    cols, rows, _ = stage_ref.shape
    for r in range(dil):
        for c in range(cols):
            lane0 = (r * cols + c) * LANES
            stage_ref[c, pl.ds(r, rows // dil, stride=dil), :] = view_ref[0, :, lane0:lane0 + LANES]
    return jnp.concatenate([stage_ref[c] for c in range(cols)], axis=1)


def _post_kernel(*refs):
    nd = len(DILATIONS)
    o_refs, l_refs = refs[0:nd], refs[nd:2 * nd]
    ob_ref, x_ref, ga_ref, gb_ref, wo_ref, gf_ref, wr_ref, x1_ref, h2_ref, afft_ref = refs[2 * nd:2 * nd + 10]
    stages = refs[2 * nd + 10:]
    lses = [_load_residue_view(ref, d, stages[2 * i]) for i, (ref, d) in enumerate(zip(l_refs, DILATIONS))]
    outs = [_load_residue_view(ref, d, stages[2 * i + 1]) for i, (ref, d) in enumerate(zip(o_refs, DILATIONS))]
    mx = functools.reduce(jnp.maximum, lses)
    ws = [jnp.exp(l - mx) for l in lses]
    add = lambda a, b: a + b
    oa = functools.reduce(add, [w * o for w, o in zip(ws, outs)]) / functools.reduce(add, ws)
    na = _rms(oa, ga_ref[...]).astype(BF16)
    nb = _rms(ob_ref[...], gb_ref[...]).astype(BF16)
    mixed = _dot(na, wo_ref[0:WIDTH_A, :]) + _dot(nb, wo_ref[WIDTH_A:WIDTH_A + WIDTH_B, :])
    x1 = x_ref[...] + mixed
    x1_ref[0] = x1
    x1_ref[1] = jnp.zeros(x1.shape, F32)
    h2 = _rms(x1, gf_ref[...])
    logits = _dot(h2.astype(BF16), wr_ref[...])
    lane = lax.broadcasted_iota(jnp.int32, logits.shape, 1)
    logits = jnp.where(lane < N_EXPERTS, logits, -jnp.inf)
    ex = jnp.exp(logits - jnp.max(logits, axis=-1, keepdims=True))
    aff = ex / jnp.sum(ex, axis=-1, keepdims=True)
    h2_ref[:, 0:D_MODEL] = h2
    h2_ref[:, D_MODEL:D_MODEL + AFF_PAD] = aff
    afft_ref[...] = aff.T[0:N_EXPERTS, :]


def _post_attention(oas, lses, ob, x2d, seq, acc_rows, norm_out_a, norm_out_b, w_out_bf, norm_ffn, w_router_bf):
    n = x2d.shape[0]
    tm = ROW_TILE
    tiles_per_seq = seq // tm
    n_tiles = n // tm
    extra = -(-(acc_rows - n) // tm)
    tile = lambda i: jnp.minimum(i, n_tiles - 1)
    row = lambda i: (tile(i), 0)
    fixed = lambda i: (0, 0)
    views = [pl.BlockSpec((1, tm // d, d * WIDTH_A), lambda i: (tile(i) // tiles_per_seq, tile(i) % tiles_per_seq, 0))
             for d in DILATIONS]
    wr_pad = jnp.zeros((D_MODEL, AFF_PAD), BF16).at[:, :N_EXPERTS].set(w_router_bf)
    return pl.pallas_call(
        _post_kernel,
        grid=(n_tiles + extra,),
        in_specs=views + views + [pl.BlockSpec((tm, WIDTH_B), row), pl.BlockSpec((tm, D_MODEL), row),
                                  pl.BlockSpec((1, WIDTH_A), fixed), pl.BlockSpec((1, WIDTH_B), fixed),
                                  pl.BlockSpec((WIDTH_A + WIDTH_B, D_MODEL), fixed),
                                  pl.BlockSpec((1, D_MODEL), fixed), pl.BlockSpec((D_MODEL, AFF_PAD), fixed)],
        out_specs=[pl.BlockSpec((2, tm, D_MODEL), lambda i: (0, i, 0)), pl.BlockSpec((tm, D_MODEL + AFF_PAD), row),
                   pl.BlockSpec((N_EXPERTS, tm), lambda i: (0, tile(i)))],
        out_shape=(jax.ShapeDtypeStruct((2, acc_rows, D_MODEL), F32), jax.ShapeDtypeStruct((n, D_MODEL + AFF_PAD), F32),
                   jax.ShapeDtypeStruct((N_EXPERTS, n), F32)),
        scratch_shapes=[pltpu.VMEM((WIDTH_A // LANES, tm, LANES), F32)] * (2 * len(DILATIONS)),
        name="post_attn",
        compiler_params=_cparams(("arbitrary",)),
    )(*oas, *lses, ob, x2d, norm_out_a.reshape(1, WIDTH_A), norm_out_b.reshape(1, WIDTH_B), w_out_bf,
      norm_ffn.reshape(1, D_MODEL), wr_pad)


def _select_kernel(aff_ref, idx_ref, *, cap):
    ne, nch, _ = aff_ref.shape
    aff = aff_ref[...]

    def count(mask):
        return jnp.sum(jnp.sum(mask.astype(F32), axis=1, keepdims=True), axis=2, keepdims=True)

    def search(i, thr_bits):
        cand = thr_bits | jnp.left_shift(jnp.int32(1), 30 - i)
        return jnp.where(count(aff >= lax.bitcast_convert_type(cand, F32)) >= cap, cand, thr_bits)

    thr = lax.bitcast_convert_type(lax.fori_loop(0, 31, search, jnp.zeros((ne, 1, 1), jnp.int32)), F32)

    ci = lax.broadcasted_iota(jnp.int32, (LANES, LANES), 0)
    cj = lax.broadcasted_iota(jnp.int32, (LANES, LANES), 1)
    upper = (ci <= cj).astype(BF16)
    ones = jnp.ones((LANES, LANES), BF16)
    ri = lax.broadcasted_iota(jnp.int32, (nch, nch), 0)
    rj = lax.broadcasted_iota(jnp.int32, (nch, nch), 1)
    strict_lower = (rj < ri).astype(BF16)

    def prefix(mask_f):
        mb = mask_f.astype(BF16).reshape(ne * nch, LANES)
        local = _dot(mb, upper).reshape(ne, nch, LANES)
        tot = _dot(mb, ones).reshape(ne, nch, LANES)
        offs = jnp.stack([_dot(strict_lower, tot[e].astype(BF16)) for e in range(ne)])
        return local, tot, offs

    gt = aff > thr
    eq = aff == thr
    need = cap - count(gt)
    eq_local, _, eq_offs = prefix(eq.astype(F32))
    sel = jnp.logical_or(gt, jnp.logical_and(eq, (eq_local + eq_offs) <= need)).astype(F32)
    loc, tot, offs = prefix(sel)

    slot = lax.broadcasted_iota(jnp.int32, (1, cap), 1).astype(F32)
    chunk_id = lax.broadcasted_iota(jnp.int32, (nch, 1), 0).astype(F32)
    for e in range(ne):
        before = offs[e][:, 0:1]
        through = before + tot[e][:, 0:1]
        chunk_of = jnp.sum((through <= slot).astype(F32), axis=0, keepdims=True)
        onehot = chunk_id == chunk_of
        base = jnp.sum(jnp.where(onehot, before, 0.0), axis=0, keepdims=True)
        rows = _dot(loc[e].T.astype(BF16), onehot.astype(BF16))
        lane_of = jnp.sum((rows <= slot - base).astype(F32), axis=0, keepdims=True)
        idx_ref[e:e + 1, :] = (chunk_of * LANES + lane_of).astype(jnp.int32)


def _select(aff_t, cap):
    ne, n = aff_t.shape
    return pl.pallas_call(
        functools.partial(_select_kernel, cap=cap),
        out_shape=jax.ShapeDtypeStruct((ne, cap), jnp.int32),
        name="ec_select",
        compiler_params=pltpu.CompilerParams(vmem_limit_bytes=VMEM_LIMIT),
    )(aff_t.reshape(ne, n // LANES, LANES))


def _expert_kernel(idx_ref, idxp_ref, idxn_ref, h2_hbm, wg_ref, wu_ref, wd_ref, acc_in_hbm, out_hbm,
                   xbuf, xe, gate, acc, gbuf, sbuf, sem_x, sem_g, sem_s, *, cap, chunk, n_tok):
    del acc_in_hbm
    e = pl.program_id(0)
    f = pl.program_id(1)
    n_exp = pl.num_programs(0)
    last = pl.num_programs(1) - 1
    capp = xbuf.shape[0]
    par = e % 2

    def x_copy(tok, j):
        return pltpu.make_async_copy(h2_hbm.at[pl.ds(jnp.minimum(tok, n_tok - 1), 1), :], xbuf.at[pl.ds(j, 1), :],
                                     sem_x)

    def g_copy(tok, j):
        return pltpu.make_async_copy(out_hbm.at[par, pl.ds(tok, 1), :], gbuf.at[pl.ds(j, 1), :], sem_g)

    def s_copy(tok, j, parity):
        return pltpu.make_async_copy(sbuf.at[pl.ds(j, 1), :], out_hbm.at[parity, pl.ds(tok, 1), :], sem_s)

    def all_rows(issue):
        def body(jo, carry):
            for u in range(8):
                issue(jo * 8 + u)
            return carry
        lax.fori_loop(0, capp // 8, body, 0)

    @pl.when(jnp.logical_and(e == 0, f == 0))
    def _first_expert():
        all_rows(lambda j: x_copy(idx_ref[0, 0, j], j).start())
        sbuf[...] = jnp.zeros(sbuf.shape, F32)

    @pl.when(f == 0)
    def _start_expert():
        pltpu.make_async_copy(h2_hbm.at[pl.ds(0, capp), :], xbuf, sem_x).wait()
        xe[...] = xbuf[0:cap, 0:D_MODEL].astype(BF16)
        aff = xbuf[0:cap, D_MODEL:D_MODEL + AFF_PAD]
        lane = lax.broadcasted_iota(jnp.int32, aff.shape, 1)
        gate[...] = jnp.sum(jnp.where(lane == e, aff, 0.0), axis=-1, keepdims=True)
        acc[...] = jnp.zeros(acc.shape, F32)

    base = f * chunk
    for u in range(chunk):
        j = base + u
        x_copy(idxn_ref[0, 0, j], j).start()
        g_copy(idx_ref[0, 0, j], j).start()
        s_copy(idxp_ref[0, 0, j], j, 1 - par).start()

    wg = wg_ref[0].astype(BF16)
    wu = wu_ref[0].astype(BF16)
    wd = wd_ref[0].astype(BF16)
    rc = min(EXPERT_ROW_CHUNK, cap)
    for r in range(cap // rc):
        rows = slice(r * rc, (r + 1) * rc)
        x = xe[rows, :]
        g = _dot(x, wg)
        hid = (g * (1.0 / (1.0 + jnp.exp(-g))) * _dot(x, wu)).astype(BF16)
        acc[rows, :] += _dot(hid, wd)

    @pl.when(f == last)
    def _finish_expert():
        pltpu.make_async_copy(out_hbm.at[0, pl.ds(0, capp), :], gbuf, sem_g).wait()
        pltpu.make_async_copy(sbuf, out_hbm.at[0, pl.ds(0, capp), :], sem_s).wait()
        sbuf[0:cap, :] = gbuf[0:cap, :] + acc[...] * gate[...]
        sbuf[cap:capp, :] = jnp.zeros((capp - cap, D_MODEL), F32)

    @pl.when(jnp.logical_and(f == last, e == n_exp - 1))
    def _last_expert():
        all_rows(lambda j: s_copy(idx_ref[0, 0, j], j, par).start())
        pltpu.make_async_copy(sbuf, out_hbm.at[0, pl.ds(0, capp), :], sem_s).wait()
        pltpu.make_async_copy(h2_hbm.at[pl.ds(0, capp), :], xbuf, sem_x).wait()


def _expert_chunking(cap):
    nf = EXPERT_FF // FF_TILE
    chunk = -(-cap // (nf * 8)) * 8
    return chunk, chunk * nf


def _experts(idx, h2_ext, acc2, n, w_gate, w_up, w_down, cap):
    nf = EXPERT_FF // FF_TILE
    chunk, capp = _expert_chunking(cap)
    pad = capp - cap
    idx_pad = jnp.concatenate([idx, jnp.broadcast_to(n + jnp.arange(pad, dtype=jnp.int32), (N_EXPERTS, pad))], axis=1)
    idx_pad = idx_pad.reshape(N_EXPERTS, 1, capp)
    idx_spec = lambda fn: pl.BlockSpec((1, 1, capp), fn, memory_space=pltpu.SMEM)
    return pl.pallas_call(
        functools.partial(_expert_kernel, cap=cap, chunk=chunk, n_tok=n),
        grid=(N_EXPERTS, nf),
        in_specs=[idx_spec(lambda e, f: (e, 0, 0)),
                  idx_spec(lambda e, f: (jnp.maximum(e - 1, 0), 0, 0)),
                  idx_spec(lambda e, f: (jnp.minimum(e + 1, N_EXPERTS - 1), 0, 0)),
                  pl.BlockSpec(memory_space=pl.ANY),
                  pl.BlockSpec((1, D_MODEL, FF_TILE), lambda e, f: (e, 0, f)),
                  pl.BlockSpec((1, D_MODEL, FF_TILE), lambda e, f: (e, 0, f)),
                  pl.BlockSpec((1, FF_TILE, D_MODEL), lambda e, f: (e, f, 0)),
                  pl.BlockSpec(memory_space=pl.ANY)],
        out_specs=pl.BlockSpec(memory_space=pl.ANY),
        out_shape=jax.ShapeDtypeStruct((2, n + pad, D_MODEL), F32),
        scratch_shapes=[pltpu.VMEM((capp, D_MODEL + AFF_PAD), F32), pltpu.VMEM((cap, D_MODEL), BF16),
                        pltpu.VMEM((cap, 1), F32), pltpu.VMEM((cap, D_MODEL), F32),
                        pltpu.VMEM((capp, D_MODEL), F32), pltpu.VMEM((capp, D_MODEL), F32),
                        pltpu.SemaphoreType.DMA(()), pltpu.SemaphoreType.DMA(()), pltpu.SemaphoreType.DMA(())],
        input_output_aliases={7: 0},
        name="experts",
        compiler_params=pltpu.CompilerParams(dimension_semantics=("arbitrary", "arbitrary"),
                                             vmem_limit_bytes=EXPERT_VMEM_LIMIT),
    )(idx_pad, idx_pad, idx_pad, h2_ext, w_gate, w_up, w_down, acc2)


def _final_kernel(a_ref, b_ref, g_ref, o_ref):
    o_ref[...] = _rms(a_ref[0] + b_ref[0], g_ref[...])


def _final_norm(acc2, n, gain):
    tm = ROW_TILE
    return pl.pallas_call(
        _final_kernel,
        grid=(n // tm,),
        in_specs=[pl.BlockSpec((1, tm, D_MODEL), lambda i: (0, i, 0)),
                  pl.BlockSpec((1, tm, D_MODEL), lambda i: (1, i, 0)),
                  pl.BlockSpec((1, D_MODEL), lambda i: (0, 0))],
        out_specs=pl.BlockSpec((tm, D_MODEL), lambda i: (i, 0)),
        out_shape=jax.ShapeDtypeStruct((n, D_MODEL), F32),
        name="final_norm",
        compiler_params=_cparams(("parallel",)),
    )(acc2, acc2, gain.reshape(1, D_MODEL))


def _trunk(x, norm_mix, w_in, q_norm_b, k_norm_b, norm_out_a, norm_out_b, w_out, norm_ffn, w_router,
           w_gate, w_up, w_down, norm_final):
    batch, seq, _ = x.shape
    n = batch * seq
    cap = EC_CAPACITY_FACTOR * n // N_EXPERTS
    x2d = x.reshape(n, D_MODEL)
    tables = _rope_tables(seq)
    assert norm_mix.shape[0] == 1, "single trunk layer"
    l = 0
    qa, ka, va, qb, kb, vb = _project(x2d, seq, norm_mix[l], w_in[l].astype(BF16), q_norm_b[l], k_norm_b[l], tables)
    pats = [_banded_attention(q, k, v, batch, seq, d) for q, k, v, d in zip(qa, ka, va, DILATIONS)]
    ob = _gqa_attention(qb, kb, vb, batch, seq)
    acc_rows = n + _expert_chunking(cap)[1] - cap
    acc2, h2_ext, aff_t = _post_attention([p[0] for p in pats], [p[1] for p in pats], ob, x2d, seq, acc_rows,
                                          norm_out_a[l], norm_out_b[l], w_out[l].astype(BF16), norm_ffn[l],
                                          w_router[l].astype(BF16))
    idx = _select(aff_t, cap)
    acc2 = _experts(idx, h2_ext, acc2, n, w_gate[l], w_up[l], w_down[l], cap)
    return _final_norm(acc2, n, norm_final).reshape(batch, seq, D_MODEL)


def kernel(x_prompt, x_sample, norm_mix, w_in, q_norm_b, k_norm_b, norm_out_a, norm_out_b, w_out, norm_ffn,
           w_router, w_gate, w_up, w_down, norm_final):
    params = (norm_mix, w_in, q_norm_b, k_norm_b, norm_out_a, norm_out_b, w_out, norm_ffn, w_router, w_gate,
              w_up, w_down, norm_final)
    return (_trunk(x_prompt, *params), _trunk(x_sample, *params))
```

```python
import functools

import jax
import jax.numpy as jnp
from jax import lax
from jax.experimental import pallas as pl
from jax.experimental.pallas import tpu as pltpu

F32 = jnp.float32
BF16 = jnp.bfloat16

D_MODEL = 1024
HEAD_DIM = 64
N_HEADS_A = 8
N_HEADS_B = 8
N_KV_B = 2
WIDTH_A = N_HEADS_A * HEAD_DIM
WIDTH_B = N_HEADS_B * HEAD_DIM
KV_WIDTH_B = N_KV_B * HEAD_DIM
IN_WIDTH = 3 * WIDTH_A + WIDTH_B + 2 * KV_WIDTH_B
DILATIONS = (1, 4, 16)
BAND_HALF = 64
ROPE_THETA_A = 500000.0
ROT_DIM_A = HEAD_DIM // 4
ROPE_THETA_B = 10000.0
GRID_W = 64
N_EXPERTS = 16
EC_CAPACITY_FACTOR = 2
EXPERT_FF = 2816
EPS = 1e-6
NEG_INF = -1e30
SCALE = HEAD_DIM ** -0.5
LOG2E = 1.4426950408889634
LN2 = 0.6931471805599453

LANES = 128
ROW_TILE = 256
Q_TILE = 128
BAND_STEP = 512
QB_TILE = 256
KB_TILE = 1024
FF_TILE = 256
AFF_PAD = LANES
EXPERT_ROW_CHUNK = 512
VMEM_LIMIT = 56 * 1024 * 1024
EXPERT_VMEM_LIMIT = 60 * 1024 * 1024


def _cparams(sem):
    return pltpu.CompilerParams(dimension_semantics=sem, vmem_limit_bytes=VMEM_LIMIT)


def _rms(x, g):
    return x * lax.rsqrt(jnp.mean(x * x, axis=-1, keepdims=True) + EPS) * g


def _dot(a, b):
    return jnp.dot(a, b, preferred_element_type=F32)


def _dot_nt(a, b):
    return lax.dot_general(a, b, (((1,), (1,)), ((), ())), preferred_element_type=F32)


def _rope(v, c, s_next, s_prev, shift):
    return v * c + pltpu.roll(v, LANES - shift, 1) * s_next + pltpu.roll(v, shift, 1) * s_prev


def _head_norm(v, gain, ones_bd):
    sq = v * v
    hi = sq.astype(BF16)
    lo = (sq - hi.astype(F32)).astype(BF16)
    ss = _dot(hi, ones_bd) + _dot(lo, ones_bd)
    return v * lax.rsqrt(ss * (1.0 / HEAD_DIM) + EPS) * gain


def _store_residue_views(stage_ref, view_refs):
    cols, rows, _ = stage_ref.shape
    for dil, ref in zip(DILATIONS, view_refs):
        for r in range(dil):
            for c in range(cols):
                part = stage_ref[c] if dil == 1 else stage_ref[c, pl.ds(r, rows // dil, stride=dil), :]
                lane0 = (r * cols + c) * LANES
                ref[0, :, lane0:lane0 + LANES] = part.astype(BF16)


def _proj_kernel(x_ref, g_ref, w_ref, qn_ref, kn_ref, ca_ref, sa1_ref, sa2_ref, cb_ref, sb1_ref, sb2_ref,
                 ones_ref, *refs):
    nd = len(DILATIONS)
    qa_refs, ka_refs, va_refs = refs[0:nd], refs[nd:2 * nd], refs[2 * nd:3 * nd]
    qb_ref, kb_ref, vb_ref, stage_ref = refs[3 * nd:]
    h = _rms(x_ref[...], g_ref[...]).astype(BF16)
    ca, sa1, sa2 = ca_ref[...], sa1_ref[...], sa2_ref[...]
    cb, sb1, sb2 = cb_ref[...], sb1_ref[...], sb2_ref[...]
    ones_bd = ones_ref[...]
    o_ka, o_va, o_qb, o_kb, o_vb = WIDTH_A, 2 * WIDTH_A, 3 * WIDTH_A, 3 * WIDTH_A + WIDTH_B, 3 * WIDTH_A + WIDTH_B + KV_WIDTH_B

    p = _dot(h, w_ref[:, 0:o_ka])
    for c in range(WIDTH_A // LANES):
        stage_ref[c] = _rope(p[:, c * LANES:(c + 1) * LANES], ca, sa1, sa2, ROT_DIM_A // 2) * (SCALE * LOG2E)
    _store_residue_views(stage_ref, qa_refs)
    p = _dot(h, w_ref[:, o_ka:o_va])
    for c in range(WIDTH_A // LANES):
        stage_ref[c] = _rope(p[:, c * LANES:(c + 1) * LANES], ca, sa1, sa2, ROT_DIM_A // 2)
    _store_residue_views(stage_ref, ka_refs)
    p = _dot(h, w_ref[:, o_va:o_qb])
    for c in range(WIDTH_A // LANES):
        stage_ref[c] = p[:, c * LANES:(c + 1) * LANES]
    _store_residue_views(stage_ref, va_refs)

    p = _dot(h, w_ref[:, o_qb:o_kb])
    qn = qn_ref[...]
    for c in range(WIDTH_B // LANES):
        sl = slice(c * LANES, (c + 1) * LANES)
        v = _head_norm(p[:, sl], qn, ones_bd)
        qb_ref[:, sl] = (_rope(v, cb, sb1, sb2, HEAD_DIM // 4) * (SCALE * LOG2E)).astype(BF16)
    p = _dot(h, w_ref[:, o_kb:o_vb])
    v = _head_norm(p, kn_ref[...], ones_bd)
    kb_ref[...] = _rope(v, cb, sb1, sb2, HEAD_DIM // 4).astype(BF16)
    vb = _dot(h, w_ref[:, o_vb:IN_WIDTH])
    lo_half = lax.broadcasted_iota(jnp.int32, vb.shape, 1) < HEAD_DIM
    vb_ref[:, 0:LANES] = jnp.where(lo_half, vb, 1.0).astype(BF16)
    vb_ref[:, LANES:2 * LANES] = jnp.where(lo_half, 1.0, vb).astype(BF16)


def _rope_tables(seq):
    lane = jnp.arange(LANES) % HEAD_DIM
    pos = jnp.arange(seq)

    def cos_sin(p, dim, theta):
        inv = theta ** (-jnp.arange(0, dim, 2, dtype=F32) / dim)
        ang = p.astype(F32)[:, None] * inv[None, :]
        return jnp.cos(ang), jnp.sin(ang)

    ha = ROT_DIM_A // 2
    cos_a, sin_a = cos_sin(pos, ROT_DIM_A, ROPE_THETA_A)
    fa = lane % ha
    in_lo, in_hi = lane < ha, (lane >= ha) & (lane < ROT_DIM_A)
    ca = jnp.where((in_lo | in_hi)[None, :], cos_a[:, fa], 1.0)
    sa1 = jnp.where(in_lo[None, :], -sin_a[:, fa], 0.0)
    sa2 = jnp.where(in_hi[None, :], sin_a[:, fa], 0.0)
    hb = HEAD_DIM // 4
    cos_r, sin_r = cos_sin(pos // GRID_W, HEAD_DIM // 2, ROPE_THETA_B)
    cos_c, sin_c = cos_sin(pos % GRID_W, HEAD_DIM // 2, ROPE_THETA_B)
    fb = lane % hb
    is_row = (lane < HEAD_DIM // 2)[None, :]
    cosb = jnp.where(is_row, cos_r[:, fb], cos_c[:, fb])
    sinb = jnp.where(is_row, sin_r[:, fb], sin_c[:, fb])
    first = ((lane // hb) % 2 == 0)[None, :]
    sb1 = jnp.where(first, -sinb, 0.0)
    sb2 = jnp.where(first, 0.0, sinb)
    return ca, sa1, sa2, cosb, sb1, sb2


def _project(x2d, seq, norm_mix, w_in_bf, q_norm, k_norm, tables):
    n = x2d.shape[0]
    tm = ROW_TILE
    tiles_per_seq = seq // tm
    row = lambda i: (i, 0)
    fixed = lambda i: (0, 0)
    tab = lambda i: (i % tiles_per_seq, 0)
    ones_bd = (jnp.arange(LANES)[:, None] // HEAD_DIM == jnp.arange(LANES)[None, :] // HEAD_DIM).astype(BF16)
    gain2 = lambda g: jnp.tile(g.reshape(1, HEAD_DIM), (1, LANES // HEAD_DIM))
    tab_spec = pl.BlockSpec((tm, LANES), tab)
    batch = n // seq
    view_shapes = [jax.ShapeDtypeStruct((batch, seq // d, d * WIDTH_A), BF16) for d in DILATIONS] * 3
    view_specs = [pl.BlockSpec((1, tm // d, d * WIDTH_A), lambda i: (i // tiles_per_seq, i % tiles_per_seq, 0))
                  for d in DILATIONS] * 3
    out_shapes = view_shapes + [
        jax.ShapeDtypeStruct((n, WIDTH_B), BF16), jax.ShapeDtypeStruct((n, KV_WIDTH_B), BF16),
        jax.ShapeDtypeStruct((n, 2 * KV_WIDTH_B), BF16)]
    outs = pl.pallas_call(
        _proj_kernel,
        grid=(n // tm,),
        in_specs=[pl.BlockSpec((tm, D_MODEL), row), pl.BlockSpec((1, D_MODEL), fixed),
                  pl.BlockSpec((D_MODEL, IN_WIDTH), fixed), pl.BlockSpec((1, LANES), fixed),
                  pl.BlockSpec((1, LANES), fixed)] + [tab_spec] * 6 + [pl.BlockSpec((LANES, LANES), fixed)],
        out_specs=view_specs + [pl.BlockSpec((tm, WIDTH_B), row), pl.BlockSpec((tm, KV_WIDTH_B), row),
                                pl.BlockSpec((tm, 2 * KV_WIDTH_B), row)],
        out_shape=out_shapes,
        scratch_shapes=[pltpu.VMEM((WIDTH_A // LANES, tm, LANES), F32)],
        name="in_proj",
        compiler_params=_cparams(("parallel",)),
    )(x2d, norm_mix.reshape(1, D_MODEL), w_in_bf, gain2(q_norm), gain2(k_norm), *tables, ones_bd)
    nd = len(DILATIONS)
    return outs[0:nd], outs[nd:2 * nd], outs[2 * nd:3 * nd], outs[3 * nd], outs[3 * nd + 1], outs[3 * nd + 2]


def _band_kernel(q_ref, kp_ref, kc_ref, kn_ref, vp_ref, vc_ref, vn_ref, o_ref, lse_ref, *, sub_len):
    i = pl.program_id(2)
    t = Q_TILE
    step = q_ref.shape[1]
    nk = t + 2 * BAND_HALF
    kall = jnp.concatenate([kp_ref[0, t - BAND_HALF:t], kc_ref[0], kn_ref[0, 0:BAND_HALF]], axis=0)
    vall = jnp.concatenate([vp_ref[0, t - BAND_HALF:t], vc_ref[0], vn_ref[0, 0:BAND_HALF]], axis=0)
    row = lax.broadcasted_iota(jnp.int32, (t, nk), 0)
    col = lax.broadcasted_iota(jnp.int32, (t, nk), 1)
    rel = col - BAND_HALF - row
    in_band = (rel >= -BAND_HALF) & (rel <= BAND_HALF)
    lo_half = lax.broadcasted_iota(jnp.int32, (t, LANES), 1) < HEAD_DIM
    zero = jnp.zeros((), BF16)
    for u in range(step // t):
        rows = slice(u * t, (u + 1) * t)
        kj = i * step + u * t - BAND_HALF + col
        valid = in_band & (kj >= 0) & (kj < sub_len)
        q, kw, vw = q_ref[0, rows, :], kall[u * t:u * t + nk], vall[u * t:u * t + nk]
        for c in range(WIDTH_A // LANES):
            sl = slice(c * LANES, (c + 1) * LANES)
            qc, kc, vc = q[:, sl], kw[:, sl], vw[:, sl]
            outs, lses = [], []
            for half_mask in (lo_half, jnp.logical_not(lo_half)):
                s = _dot_nt(jnp.where(half_mask, qc, zero), kc)
                s = jnp.where(valid, s, NEG_INF)
                m = jnp.max(s, axis=-1, keepdims=True)
                p = jnp.exp2(s - m)
                l = jnp.sum(p, axis=-1, keepdims=True)
                outs.append(_dot(p.astype(BF16), vc) / l)
                lses.append(jnp.broadcast_to(m * LN2 + jnp.log(l), (t, LANES)))
            o_ref[0, rows, sl] = jnp.where(lo_half, outs[0], outs[1])
            lse_ref[0, rows, sl] = jnp.where(lo_half, lses[0], lses[1])


def _banded_attention(q, k, v, batch, seq, dil):
    sub_len = seq // dil
    step = min(BAND_STEP, sub_len)
    nb = sub_len // step
    per = step // Q_TILE
    halo = (1, Q_TILE, WIDTH_A)
    cur = pl.BlockSpec((1, step, WIDTH_A), lambda b, r, i: (b, i, r))
    prev = pl.BlockSpec(halo, lambda b, r, i: (b, jnp.maximum(per * i - 1, 0), r))
    nxt = pl.BlockSpec(halo, lambda b, r, i: (b, jnp.minimum(per * (i + 1), per * nb - 1), r))
    shape = jax.ShapeDtypeStruct((batch, sub_len, dil * WIDTH_A), F32)
    o, lse = pl.pallas_call(
        functools.partial(_band_kernel, sub_len=sub_len),
        grid=(batch, dil, nb),
        in_specs=[cur, prev, cur, nxt, prev, cur, nxt],
        out_specs=[cur, cur],
        out_shape=(shape, shape),
        name=f"band_attn_d{dil}",
        compiler_params=_cparams(("parallel", "parallel", "parallel")),
    )(q, k, k, k, v, v, v)
    return o, lse


def _gqa_kernel(q_ref, k_ref, v_ref, o_ref, qs_ref, m_ref, acc_ref, s0_ref, s1_ref, *, seq):
    tq = QB_TILE
    group = N_HEADS_B // N_KV_B
    lane_half = lax.broadcasted_iota(jnp.int32, (tq, LANES), 1) // HEAD_DIM
    heads_out = [None] * N_HEADS_B
    for n in range(N_KV_B):
        for g in range(group):
            hd = n * group + g
            qc = q_ref[0, :, (hd // 2) * LANES:(hd // 2 + 1) * LANES].astype(F32)
            if hd % 2 != n:
                qc = pltpu.roll(qc, HEAD_DIM, 1)
            qs_ref[g * tq:(g + 1) * tq, :] = jnp.where(lane_half == n, qc, 0.0).astype(BF16)
        m_ref[...] = jnp.full(m_ref.shape, -jnp.inf, F32)
        acc_ref[...] = jnp.zeros(acc_ref.shape, F32)

        def scores(j, dst_ref):
            start = pl.multiple_of(j * KB_TILE, KB_TILE)
            dst_ref[...] = _dot_nt(qs_ref[...], k_ref[0, pl.ds(start, KB_TILE), :])

        def accumulate(j, src_ref):
            start = pl.multiple_of(j * KB_TILE, KB_TILE)
            vc = v_ref[0, pl.ds(start, KB_TILE), n * LANES:(n + 1) * LANES]
            s = src_ref[...]
            m_old = m_ref[...]
            m_new = jnp.maximum(m_old, jnp.max(s, axis=-1, keepdims=True))
            p = jnp.exp2(s - jnp.concatenate([m_new] * (KB_TILE // LANES), axis=1))
            acc_ref[...] = jnp.exp2(m_old - m_new) * acc_ref[...] + _dot(p.astype(BF16), vc)
            m_ref[...] = m_new

        def kv_pair(i, carry):
            j = 2 * i
            scores(j + 1, s1_ref)
            accumulate(j, s0_ref)
            scores(j + 2, s0_ref)
            accumulate(j + 1, s1_ref)
            return carry

        n_chunks = seq // KB_TILE
        scores(0, s0_ref)
        lax.fori_loop(0, n_chunks // 2 - 1, kv_pair, 0)
        scores(n_chunks - 1, s1_ref)
        accumulate(n_chunks - 2, s0_ref)
        accumulate(n_chunks - 1, s1_ref)
        acc = acc_ref[...]
        res = acc / pltpu.roll(acc, HEAD_DIM, 1)
        for g in range(group):
            heads_out[n * group + g] = res[g * tq:(g + 1) * tq, :]
    for c in range(WIDTH_B // LANES):
        even, odd = heads_out[2 * c], heads_out[2 * c + 1]
        n = (2 * c) // group
        if n == 0:
            odd = pltpu.roll(odd, HEAD_DIM, 1)
        else:
            even = pltpu.roll(even, HEAD_DIM, 1)
        o_ref[0, :, c * LANES:(c + 1) * LANES] = jnp.where(lane_half == 0, even, odd)


def _gqa_attention(q, k, v, batch, seq):
    tq = QB_TILE
    rows = (N_HEADS_B // N_KV_B) * tq
    o = pl.pallas_call(
        functools.partial(_gqa_kernel, seq=seq),
        grid=(batch, seq // tq),
        in_specs=[pl.BlockSpec((1, tq, WIDTH_B), lambda b, i: (b, i, 0)),
                  pl.BlockSpec((1, seq, KV_WIDTH_B), lambda b, i: (b, 0, 0)),
                  pl.BlockSpec((1, seq, 2 * KV_WIDTH_B), lambda b, i: (b, 0, 0))],
        out_specs=pl.BlockSpec((1, tq, WIDTH_B), lambda b, i: (b, i, 0)),
        out_shape=jax.ShapeDtypeStruct((batch, seq, WIDTH_B), F32),
        scratch_shapes=[pltpu.VMEM((rows, LANES), BF16), pltpu.VMEM((rows, LANES), F32),
                        pltpu.VMEM((rows, LANES), F32), pltpu.VMEM((rows, KB_TILE), F32),
                        pltpu.VMEM((rows, KB_TILE), F32)],
        name="gqa_attn",
        compiler_params=_cparams(("parallel", "parallel")),
    )(q.reshape(batch, seq, WIDTH_B), k.reshape(batch, seq, KV_WIDTH_B), v.reshape(batch, seq, 2 * KV_WIDTH_B))
    return o.reshape(batch * seq, WIDTH_B)


def _load_residue_view(view_ref, dil, stage_ref):
    if dil == 1:
        return view_ref[0]
    cols, rows, _ = stage_ref.shape
    for r in range(dil):
        for c in range(cols):
            lane0 = (r * cols + c) * LANES
            stage_ref[c, pl.ds(r, rows // dil, stride=dil), :] = view_ref[0, :, lane0:lane0 + LANES]
    return jnp.concatenate([stage_ref[c] for c in range(cols)], axis=1)


def _post_kernel(*refs):
    nd = len(DILATIONS)
    o_refs, l_refs = refs[0:nd], refs[nd:2 * nd]
    ob_ref, x_ref, ga_ref, gb_ref, wo_ref, gf_ref, wr_ref, x1_ref, h2_ref, afft_ref = refs[2 * nd:2 * nd + 10]
    stages = refs[2 * nd + 10:]
    lses = [_load_residue_view(ref, d, stages[2 * i]) for i, (ref, d) in enumerate(zip(l_refs, DILATIONS))]
    outs = [_load_residue_view(ref, d, stages[2 * i + 1]) for i, (ref, d) in enumerate(zip(o_refs, DILATIONS))]
    mx = functools.reduce(jnp.maximum, lses)
    ws = [jnp.exp(l - mx) for l in lses]
    add = lambda a, b: a + b
    oa = functools.reduce(add, [w * o for w, o in zip(ws, outs)]) / functools.reduce(add, ws)
    na = _rms(oa, ga_ref[...]).astype(BF16)
    nb = _rms(ob_ref[...], gb_ref[...]).astype(BF16)
    mixed = _dot(na, wo_ref[0:WIDTH_A, :]) + _dot(nb, wo_ref[WIDTH_A:WIDTH_A + WIDTH_B, :])
    x1 = x_ref[...] + mixed
    x1_ref[0] = x1
    x1_ref[1] = jnp.zeros(x1.shape, F32)
    h2 = _rms(x1, gf_ref[...])
    logits = _dot(h2.astype(BF16), wr_ref[...])
    lane = lax.broadcasted_iota(jnp.int32, logits.shape, 1)
    logits = jnp.where(lane < N_EXPERTS, logits, -jnp.inf)
    ex = jnp.exp(logits - jnp.max(logits, axis=-1, keepdims=True))
    aff = ex / jnp.sum(ex, axis=-1, keepdims=True)
    h2_ref[:, 0:D_MODEL] = h2
    h2_ref[:, D_MODEL:D_MODEL + AFF_PAD] = aff
    afft_ref[...] = aff.T[0:N_EXPERTS, :]


def _post_attention(oas, lses, ob, x2d, seq, acc_rows, norm_out_a, norm_out_b, w_out_bf, norm_ffn, w_router_bf):
    n = x2d.shape[0]
    tm = ROW_TILE
    tiles_per_seq = seq // tm
    n_tiles = n // tm
    extra = -(-(acc_rows - n) // tm)
    tile = lambda i: jnp.minimum(i, n_tiles - 1)
    row = lambda i: (tile(i), 0)
    fixed = lambda i: (0, 0)
    views = [pl.BlockSpec((1, tm // d, d * WIDTH_A), lambda i: (tile(i) // tiles_per_seq, tile(i) % tiles_per_seq, 0))
             for d in DILATIONS]
    wr_pad = jnp.zeros((D_MODEL, AFF_PAD), BF16).at[:, :N_EXPERTS].set(w_router_bf)
    return pl.pallas_call(
        _post_kernel,
        grid=(n_tiles + extra,),
        in_specs=views + views + [pl.BlockSpec((tm, WIDTH_B), row), pl.BlockSpec((tm, D_MODEL), row),
                                  pl.BlockSpec((1, WIDTH_A), fixed), pl.BlockSpec((1, WIDTH_B), fixed),
                                  pl.BlockSpec((WIDTH_A + WIDTH_B, D_MODEL), fixed),
                                  pl.BlockSpec((1, D_MODEL), fixed), pl.BlockSpec((D_MODEL, AFF_PAD), fixed)],
        out_specs=[pl.BlockSpec((2, tm, D_MODEL), lambda i: (0, i, 0)), pl.BlockSpec((tm, D_MODEL + AFF_PAD), row),
                   pl.BlockSpec((N_EXPERTS, tm), lambda i: (0, tile(i)))],
        out_shape=(jax.ShapeDtypeStruct((2, acc_rows, D_MODEL), F32), jax.ShapeDtypeStruct((n, D_MODEL + AFF_PAD), F32),
                   jax.ShapeDtypeStruct((N_EXPERTS, n), F32)),
        scratch_shapes=[pltpu.VMEM((WIDTH_A // LANES, tm, LANES), F32)] * (2 * len(DILATIONS)),
        name="post_attn",
        compiler_params=_cparams(("arbitrary",)),
    )(*oas, *lses, ob, x2d, norm_out_a.reshape(1, WIDTH_A), norm_out_b.reshape(1, WIDTH_B), w_out_bf,
      norm_ffn.reshape(1, D_MODEL), wr_pad)


def _select_kernel(aff_ref, idx_ref, *, cap):
    ne, nch, _ = aff_ref.shape
    aff = aff_ref[...]

    def count(mask):
        return jnp.sum(jnp.sum(mask.astype(F32), axis=1, keepdims=True), axis=2, keepdims=True)

    def search(i, thr_bits):
        cand = thr_bits | jnp.left_shift(jnp.int32(1), 30 - i)
        return jnp.where(count(aff >= lax.bitcast_convert_type(cand, F32)) >= cap, cand, thr_bits)

    thr = lax.bitcast_convert_type(lax.fori_loop(0, 31, search, jnp.zeros((ne, 1, 1), jnp.int32)), F32)

    ci = lax.broadcasted_iota(jnp.int32, (LANES, LANES), 0)
    cj = lax.broadcasted_iota(jnp.int32, (LANES, LANES), 1)
    upper = (ci <= cj).astype(BF16)
    ones = jnp.ones((LANES, LANES), BF16)
    ri = lax.broadcasted_iota(jnp.int32, (nch, nch), 0)
    rj = lax.broadcasted_iota(jnp.int32, (nch, nch), 1)
    strict_lower = (rj < ri).astype(BF16)

    def prefix(mask_f):
        mb = mask_f.astype(BF16).reshape(ne * nch, LANES)
        local = _dot(mb, upper).reshape(ne, nch, LANES)
        tot = _dot(mb, ones).reshape(ne, nch, LANES)
        offs = jnp.stack([_dot(strict_lower, tot[e].astype(BF16)) for e in range(ne)])
        return local, tot, offs

    gt = aff > thr
    eq = aff == thr
    need = cap - count(gt)
    eq_local, _, eq_offs = prefix(eq.astype(F32))
    sel = jnp.logical_or(gt, jnp.logical_and(eq, (eq_local + eq_offs) <= need)).astype(F32)
    loc, tot, offs = prefix(sel)

    slot = lax.broadcasted_iota(jnp.int32, (1, cap), 1).astype(F32)
    chunk_id = lax.broadcasted_iota(jnp.int32, (nch, 1), 0).astype(F32)
    for e in range(ne):
        before = offs[e][:, 0:1]
        through = before + tot[e][:, 0:1]
        chunk_of = jnp.sum((through <= slot).astype(F32), axis=0, keepdims=True)
        onehot = chunk_id == chunk_of
        base = jnp.sum(jnp.where(onehot, before, 0.0), axis=0, keepdims=True)
        rows = _dot(loc[e].T.astype(BF16), onehot.astype(BF16))
        lane_of = jnp.sum((rows <= slot - base).astype(F32), axis=0, keepdims=True)
        idx_ref[e:e + 1, :] = (chunk_of * LANES + lane_of).astype(jnp.int32)


def _select(aff_t, cap):
    ne, n = aff_t.shape
    return pl.pallas_call(
        functools.partial(_select_kernel, cap=cap),
        out_shape=jax.ShapeDtypeStruct((ne, cap), jnp.int32),
        name="ec_select",
        compiler_params=pltpu.CompilerParams(vmem_limit_bytes=VMEM_LIMIT),
    )(aff_t.reshape(ne, n // LANES, LANES))


def _expert_kernel(idx_ref, idxp_ref, idxn_ref, h2_hbm, wg_ref, wu_ref, wd_ref, acc_in_hbm, out_hbm,
                   xbuf, xe, gate, acc, gbuf, sbuf, sem_x, sem_g, sem_s, *, cap, chunk, n_tok):
    del acc_in_hbm
    e = pl.program_id(0)
    f = pl.program_id(1)
    n_exp = pl.num_programs(0)
    last = pl.num_programs(1) - 1
    capp = xbuf.shape[0]
    par = e % 2

    def x_copy(tok, j):
        return pltpu.make_async_copy(h2_hbm.at[pl.ds(jnp.minimum(tok, n_tok - 1), 1), :], xbuf.at[pl.ds(j, 1), :],
                                     sem_x)

    def g_copy(tok, j):
        return pltpu.make_async_copy(out_hbm.at[par, pl.ds(tok, 1), :], gbuf.at[pl.ds(j, 1), :], sem_g)

    def s_copy(tok, j, parity):
        return pltpu.make_async_copy(sbuf.at[pl.ds(j, 1), :], out_hbm.at[parity, pl.ds(tok, 1), :], sem_s)

    def all_rows(issue):
        def body(jo, carry):
            for u in range(8):
                issue(jo * 8 + u)
            return carry
        lax.fori_loop(0, capp // 8, body, 0)

    @pl.when(jnp.logical_and(e == 0, f == 0))
    def _first_expert():
        all_rows(lambda j: x_copy(idx_ref[0, 0, j], j).start())
        sbuf[...] = jnp.zeros(sbuf.shape, F32)

    @pl.when(f == 0)
    def _start_expert():
        pltpu.make_async_copy(h2_hbm.at[pl.ds(0, capp), :], xbuf, sem_x).wait()
        xe[...] = xbuf[0:cap, 0:D_MODEL].astype(BF16)
        aff = xbuf[0:cap, D_MODEL:D_MODEL + AFF_PAD]
        lane = lax.broadcasted_iota(jnp.int32, aff.shape, 1)
        gate[...] = jnp.sum(jnp.where(lane == e, aff, 0.0), axis=-1, keepdims=True)
        acc[...] = jnp.zeros(acc.shape, F32)

    base = f * chunk
    for u in range(chunk):
        j = base + u
        x_copy(idxn_ref[0, 0, j], j).start()
        g_copy(idx_ref[0, 0, j], j).start()
        s_copy(idxp_ref[0, 0, j], j, 1 - par).start()

    wg = wg_ref[0].astype(BF16)
    wu = wu_ref[0].astype(BF16)
    wd = wd_ref[0].astype(BF16)
    rc = min(EXPERT_ROW_CHUNK, cap)
    for r in range(cap // rc):
        rows = slice(r * rc, (r + 1) * rc)
        x = xe[rows, :]
        g = _dot(x, wg)
        hid = (g * (1.0 / (1.0 + jnp.exp(-g))) * _dot(x, wu)).astype(BF16)
        acc[rows, :] += _dot(hid, wd)

    @pl.when(f == last)
    def _finish_expert():
        pltpu.make_async_copy(out_hbm.at[0, pl.ds(0, capp), :], gbuf, sem_g).wait()
        pltpu.make_async_copy(sbuf, out_hbm.at[0, pl.ds(0, capp), :], sem_s).wait()
        sbuf[0:cap, :] = gbuf[0:cap, :] + acc[...] * gate[...]
        sbuf[cap:capp, :] = jnp.zeros((capp - cap, D_MODEL), F32)

    @pl.when(jnp.logical_and(f == last, e == n_exp - 1))
    def _last_expert():
        all_rows(lambda j: s_copy(idx_ref[0, 0, j], j, par).start())
        pltpu.make_async_copy(sbuf, out_hbm.at[0, pl.ds(0, capp), :], sem_s).wait()
        pltpu.make_async_copy(h2_hbm.at[pl.ds(0, capp), :], xbuf, sem_x).wait()


def _expert_chunking(cap):
    nf = EXPERT_FF // FF_TILE
    chunk = -(-cap // (nf * 8)) * 8
    return chunk, chunk * nf


def _experts(idx, h2_ext, acc2, n, w_gate, w_up, w_down, cap):
    nf = EXPERT_FF // FF_TILE
    chunk, capp = _expert_chunking(cap)
    pad = capp - cap
    idx_pad = jnp.concatenate([idx, jnp.broadcast_to(n + jnp.arange(pad, dtype=jnp.int32), (N_EXPERTS, pad))], axis=1)
    idx_pad = idx_pad.reshape(N_EXPERTS, 1, capp)
    idx_spec = lambda fn: pl.BlockSpec((1, 1, capp), fn, memory_space=pltpu.SMEM)
    return pl.pallas_call(
        functools.partial(_expert_kernel, cap=cap, chunk=chunk, n_tok=n),
        grid=(N_EXPERTS, nf),
        in_specs=[idx_spec(lambda e, f: (e, 0, 0)),
                  idx_spec(lambda e, f: (jnp.maximum(e - 1, 0), 0, 0)),
                  idx_spec(lambda e, f: (jnp.minimum(e + 1, N_EXPERTS - 1), 0, 0)),
                  pl.BlockSpec(memory_space=pl.ANY),
                  pl.BlockSpec((1, D_MODEL, FF_TILE), lambda e, f: (e, 0, f)),
                  pl.BlockSpec((1, D_MODEL, FF_TILE), lambda e, f: (e, 0, f)),
                  pl.BlockSpec((1, FF_TILE, D_MODEL), lambda e, f: (e, f, 0)),
                  pl.BlockSpec(memory_space=pl.ANY)],
        out_specs=pl.BlockSpec(memory_space=pl.ANY),
        out_shape=jax.ShapeDtypeStruct((2, n + pad, D_MODEL), F32),
        scratch_shapes=[pltpu.VMEM((capp, D_MODEL + AFF_PAD), F32), pltpu.VMEM((cap, D_MODEL), BF16),
                        pltpu.VMEM((cap, 1), F32), pltpu.VMEM((cap, D_MODEL), F32),
                        pltpu.VMEM((capp, D_MODEL), F32), pltpu.VMEM((capp, D_MODEL), F32),
                        pltpu.SemaphoreType.DMA(()), pltpu.SemaphoreType.DMA(()), pltpu.SemaphoreType.DMA(())],
        input_output_aliases={7: 0},
        name="experts",
        compiler_params=pltpu.CompilerParams(dimension_semantics=("arbitrary", "arbitrary"),
                                             vmem_limit_bytes=EXPERT_VMEM_LIMIT),
    )(idx_pad, idx_pad, idx_pad, h2_ext, w_gate, w_up, w_down, acc2)


def _final_kernel(a_ref, b_ref, g_ref, o_ref):
    o_ref[...] = _rms(a_ref[0] + b_ref[0], g_ref[...])


def _final_norm(acc2, n, gain):
    tm = ROW_TILE
    return pl.pallas_call(
        _final_kernel,
        grid=(n // tm,),
        in_specs=[pl.BlockSpec((1, tm, D_MODEL), lambda i: (0, i, 0)),
                  pl.BlockSpec((1, tm, D_MODEL), lambda i: (1, i, 0)),
                  pl.BlockSpec((1, D_MODEL), lambda i: (0, 0))],
        out_specs=pl.BlockSpec((tm, D_MODEL), lambda i: (i, 0)),
        out_shape=jax.ShapeDtypeStruct((n, D_MODEL), F32),
        name="final_norm",
        compiler_params=_cparams(("parallel",)),
    )(acc2, acc2, gain.reshape(1, D_MODEL))


def _trunk(x, norm_mix, w_in, q_norm_b, k_norm_b, norm_out_a, norm_out_b, w_out, norm_ffn, w_router,
           w_gate, w_up, w_down, norm_final):
    batch, seq, _ = x.shape
    n = batch * seq
    cap = EC_CAPACITY_FACTOR * n // N_EXPERTS
    x2d = x.reshape(n, D_MODEL)
    tables = _rope_tables(seq)
    assert norm_mix.shape[0] == 1, "single trunk layer"
    l = 0
    qa, ka, va, qb, kb, vb = _project(x2d, seq, norm_mix[l], w_in[l].astype(BF16), q_norm_b[l], k_norm_b[l], tables)
    pats = [_banded_attention(q, k, v, batch, seq, d) for q, k, v, d in zip(qa, ka, va, DILATIONS)]
    ob = _gqa_attention(qb, kb, vb, batch, seq)
    acc_rows = n + _expert_chunking(cap)[1] - cap
    acc2, h2_ext, aff_t = _post_attention([p[0] for p in pats], [p[1] for p in pats], ob, x2d, seq, acc_rows,
                                          norm_out_a[l], norm_out_b[l], w_out[l].astype(BF16), norm_ffn[l],
                                          w_router[l].astype(BF16))
    idx = _select(aff_t, cap)
    acc2 = _experts(idx, h2_ext, acc2, n, w_gate[l], w_up[l], w_down[l], cap)
    return _final_norm(acc2, n, norm_final).reshape(batch, seq, D_MODEL)


def kernel(x_prompt, x_sample, norm_mix, w_in, q_norm_b, k_norm_b, norm_out_a, norm_out_b, w_out, norm_ffn,
           w_router, w_gate, w_up, w_down, norm_final):
    params = (norm_mix, w_in, q_norm_b, k_norm_b, norm_out_a, norm_out_b, w_out, norm_ffn, w_router, w_gate,
              w_up, w_down, norm_final)
    return (_trunk(x_prompt, *params), _trunk(x_sample, *params))
```
